```python
import jax, jax.numpy as jnp
from jax import lax
import numpy as np

D_MODEL = 1024
BATCH = 8
SEQ = 2048
DEPTH = 2

PLE_DIM = 256
HEAD_DIM = 64
SB_WIDTH = D_MODEL // 4
SB_HEADS = SB_WIDTH // HEAD_DIM
MLA_WIDTH = D_MODEL // 2
MLA_HEADS = MLA_WIDTH // HEAD_DIM
MLA_NOPE_DIM = 64
MLA_ROPE_DIM = 32
MLA_V_DIM = HEAD_DIM
MLA_Q_RANK = 384
MLA_KV_RANK = 256
CONV_WIDTH = D_MODEL // 4
CONV_K = 3
MIX_WIDTH = SB_WIDTH + MLA_WIDTH + CONV_WIDTH
IN_SIZES = (SB_WIDTH, SB_WIDTH, SB_WIDTH,
            MLA_Q_RANK, MLA_KV_RANK, MLA_ROPE_DIM,
            CONV_WIDTH, CONV_WIDTH, CONV_WIDTH)
IN_WIDTH = sum(IN_SIZES)
D_FF = 2816
Q_BLOCK = 128
ROPE_BASE = 10000.0
EPS = 1e-6
NEG_INF = -1e30

kernel_name = "hybrid_sb_mla_shortconv_macaron"


def rmsnorm(x, g):
    xf = x.astype(jnp.float32)
    y = xf * lax.rsqrt(jnp.mean(xf * xf, axis=-1, keepdims=True) + EPS)
    return (y * g.astype(jnp.float32)).astype(x.dtype)


def swiglu(x, w_gate, w_up, w_down):
    return (jax.nn.silu(x @ w_gate) * (x @ w_up)) @ w_down


def rope(x, pos):
    half = x.shape[-1] // 2
    inv = ROPE_BASE ** (-jnp.arange(half, dtype=jnp.float32) / half)
    ang = pos.astype(jnp.float32)[..., None] * inv
    cos = jnp.cos(ang)[:, :, None, :]
    sin = jnp.sin(ang)[:, :, None, :]
    xf = x.astype(jnp.float32)
    x1, x2 = xf[..., :half], xf[..., half:]
    return jnp.concatenate([x1 * cos - x2 * sin, x2 * cos + x1 * sin], axis=-1).astype(x.dtype)


def stick_breaking_attention(q, k, v):
    S = q.shape[1]
    scale = q.shape[-1] ** -0.5
    outs = []
    for start in range(0, S, Q_BLOCK):
        end = start + Q_BLOCK
        z = jnp.einsum("bqhd,bkhd->bhqk", q[:, start:end], k[:, :end],
                       preferred_element_type=jnp.float32) * scale
        t_idx = start + jnp.arange(Q_BLOCK)[:, None]
        s_idx = jnp.arange(end)[None, :]
        mask = s_idx < t_idx
        log_not = jnp.where(mask, -jax.nn.softplus(z), 0.0)
        after = lax.cumsum(log_not, axis=3, reverse=True) - log_not
        w = jnp.where(mask, jnp.exp(jax.nn.log_sigmoid(z) + after), 0.0)
        outs.append(jnp.einsum("bhqk,bkhd->bqhd", w.astype(v.dtype), v[:, :end]))
    return jnp.concatenate(outs, axis=1)


def latent_attention(c_q, c_kv, k_rope_raw, pos, g_q, g_kv, w_uq, w_ukv):
    B, S = c_q.shape[0], c_q.shape[1]
    q = (rmsnorm(c_q, g_q) @ w_uq).reshape(B, S, MLA_HEADS, MLA_NOPE_DIM + MLA_ROPE_DIM)
    q_nope, q_rope = q[..., :MLA_NOPE_DIM], rope(q[..., MLA_NOPE_DIM:], pos)
    kv = (rmsnorm(c_kv, g_kv) @ w_ukv).reshape(B, S, MLA_HEADS, MLA_NOPE_DIM + MLA_V_DIM)
    k_nope, v = kv[..., :MLA_NOPE_DIM], kv[..., MLA_NOPE_DIM:]
    k_rope = rope(k_rope_raw[:, :, None, :], pos)[:, :, 0, :]
    scale = (MLA_NOPE_DIM + MLA_ROPE_DIM) ** -0.5
    outs = []
    for start in range(0, S, Q_BLOCK):
        end = start + Q_BLOCK
        sc = (jnp.einsum("bqhd,bkhd->bhqk", q_nope[:, start:end], k_nope[:, :end],
                         preferred_element_type=jnp.float32)
              + jnp.einsum("bqhr,bkr->bhqk", q_rope[:, start:end], k_rope[:, :end],
                           preferred_element_type=jnp.float32)) * scale
        mask = jnp.arange(end)[None, :] <= (start + jnp.arange(Q_BLOCK)[:, None])
        w = jax.nn.softmax(jnp.where(mask, sc, NEG_INF), axis=-1)
        outs.append(jnp.einsum("bhqk,bkhd->bqhd", w.astype(v.dtype), v[:, :end]))
    return jnp.concatenate(outs, axis=1)


def short_gated_conv(b_gate, c_gate, h, w_conv):
    u = c_gate * h
    y = lax.conv_general_dilated(u, w_conv[:, None, :].astype(u.dtype), window_strides=(1,),
                                 padding=[(CONV_K - 1, 0)],
                                 dimension_numbers=("NWC", "WIO", "NWC"),
                                 feature_group_count=u.shape[-1])
    return b_gate * y


def setup_inputs(seed: int = 0) -> dict:
    key = jax.random.key(seed)
    ks = iter(jax.random.split(key, 32))

    def dense(shape, fan_in):
        return jax.random.normal(next(ks), shape, jnp.float32) * fan_in ** -0.5

    def gain(dim):
        return 1.0 + 0.05 * jax.random.normal(next(ks), (DEPTH, dim), jnp.float32)

    x = jax.random.normal(next(ks), (BATCH, SEQ, D_MODEL), jnp.float32)
    p = jax.random.normal(next(ks), (DEPTH, BATCH, SEQ, PLE_DIM), jnp.float32)
    offsets = jax.random.randint(next(ks), (BATCH, 1), 0, 1024, dtype=jnp.int32)
    positions = offsets + jnp.arange(SEQ, dtype=jnp.int32)[None, :]
    return {
        "x": x, "p": p, "positions": positions,
        "g_ffn1_pre": gain(D_MODEL),
        "w1_gate": dense((DEPTH, D_MODEL, D_FF), D_MODEL),
        "w1_up": dense((DEPTH, D_MODEL, D_FF), D_MODEL),
        "w1_down": dense((DEPTH, D_FF, D_MODEL), D_FF),
        "g_ffn1_post": gain(D_MODEL),
        "g_mix_pre": gain(D_MODEL),
        "w_in": dense((DEPTH, D_MODEL, IN_WIDTH), D_MODEL),
        "g_mla_q": gain(MLA_Q_RANK),
        "w_mla_uq": dense((DEPTH, MLA_Q_RANK, MLA_HEADS * (MLA_NOPE_DIM + MLA_ROPE_DIM)), MLA_Q_RANK),
        "g_mla_kv": gain(MLA_KV_RANK),
        "w_mla_ukv": dense((DEPTH, MLA_KV_RANK, MLA_HEADS * (MLA_NOPE_DIM + MLA_V_DIM)), MLA_KV_RANK),
        "w_conv": dense((DEPTH, CONV_K, CONV_WIDTH), CONV_K),
        "w_out": dense((DEPTH, MIX_WIDTH, D_MODEL), MIX_WIDTH),
        "g_mix_post": gain(D_MODEL),
        "g_ffn2_pre": gain(D_MODEL),
        "w2_gate": dense((DEPTH, D_MODEL, D_FF), D_MODEL),
        "w2_up": dense((DEPTH, D_MODEL, D_FF), D_MODEL),
        "w2_down": dense((DEPTH, D_FF, D_MODEL), D_FF),
        "g_ffn2_post": gain(D_MODEL),
        "g_ple_pre": gain(D_MODEL),
        "w_ple_gate": dense((DEPTH, D_MODEL, D_MODEL), D_MODEL),
        "w_ple_proj": dense((DEPTH, PLE_DIM, D_MODEL), PLE_DIM),
        "g_ple_post": gain(D_MODEL),
    }


def reference(x, p, positions, g_ffn1_pre, w1_gate, w1_up, w1_down, g_ffn1_post,
              g_mix_pre, w_in, g_mla_q, w_mla_uq, g_mla_kv, w_mla_ukv, w_conv, w_out,
              g_mix_post, g_ffn2_pre, w2_gate, w2_up, w2_down, g_ffn2_post,
              g_ple_pre, w_ple_gate, w_ple_proj, g_ple_post):
    B, S = x.shape[0], x.shape[1]
    split_at = [int(i) for i in np.cumsum(IN_SIZES)[:-1]]
    for i in range(DEPTH):
        h = rmsnorm(x, g_ffn1_pre[i])
        x = x + 0.5 * rmsnorm(swiglu(h, w1_gate[i], w1_up[i], w1_down[i]), g_ffn1_post[i])

        h = rmsnorm(x, g_mix_pre[i])
        (sb_q, sb_k, sb_v, c_q, c_kv, k_rope_raw,
         cv_b, cv_c, cv_h) = jnp.split(h @ w_in[i], split_at, axis=-1)
        heads = lambda t: t.reshape(B, S, SB_HEADS, HEAD_DIM)
        y_sb = stick_breaking_attention(heads(sb_q), heads(sb_k), heads(sb_v))
        y_mla = latent_attention(c_q, c_kv, k_rope_raw, positions, g_mla_q[i], g_mla_kv[i],
                                 w_mla_uq[i], w_mla_ukv[i])
        y_cv = short_gated_conv(cv_b, cv_c, cv_h, w_conv[i])
        mixed = jnp.concatenate([y_sb.reshape(B, S, SB_WIDTH),
                                 y_mla.reshape(B, S, MLA_WIDTH), y_cv], axis=-1) @ w_out[i]
        x = x + rmsnorm(mixed, g_mix_post[i])

        h = rmsnorm(x, g_ffn2_pre[i])
        x = x + 0.5 * rmsnorm(swiglu(h, w2_gate[i], w2_up[i], w2_down[i]), g_ffn2_post[i])

        h = rmsnorm(x, g_ple_pre[i])
        e = jax.nn.sigmoid(h @ w_ple_gate[i]) * (p[i].astype(x.dtype) @ w_ple_proj[i])
        x = x + rmsnorm(e, g_ple_post[i])
    return x
```

```python
import functools

import jax
import jax.numpy as jnp
from jax import lax
from jax.experimental import pallas as pl
from jax.experimental.pallas import tpu as pltpu

F32 = jnp.float32
BF16 = jnp.bfloat16

D_MODEL = 1024
D_FF = 2816
PLE_DIM = 256
HEAD_DIM = 64
SB_WIDTH = 256
MLA_HEADS = 8
MLA_NOPE = 64
MLA_ROPE = 32
MLA_Q_RANK = 384
MLA_KV_RANK = 256
CONV_WIDTH = 256
CONV_K = 3
ROPE_BASE = 10000.0
EPS = 1e-6

LANES = 128
FF_CHUNK = 256
N_FF_CHUNKS = D_FF // FF_CHUNK
TOKEN_TILE = 512
ATT_BLOCK = 256
VMEM_LIMIT = 52 * 1024 * 1024

_SB_Q, _SB_K, _SB_V = 0, 256, 512
_C_Q = 768
_C_KV = _C_Q + MLA_Q_RANK
_CV = _C_KV + MLA_KV_RANK
_KR = _CV + 3 * CONV_WIDTH
IN_PACKED = _KR + LANES


def _rms(x, g):
    return x * lax.rsqrt(jnp.mean(x * x, axis=-1, keepdims=True) + EPS) * g


def _dot(a, b):
    return jnp.dot(a, b, preferred_element_type=F32)


def _dot_nt(a, b):
    return lax.dot_general(a, b, (((1,), (1,)), ((), ())), preferred_element_type=F32)


def _const_spec(shape):
    zeros = (0,) * len(shape)
    return pl.BlockSpec(shape, lambda *_: zeros, pipeline_mode=pl.Buffered(1))


def _row_spec(tile, width):
    return pl.BlockSpec((tile, width), lambda i: (i, 0))


def _rope_table_kernel(pos_ref, inv_ref, sgn_ref, cos_ref, sin_ref):
    ang = pos_ref[...].astype(F32) * inv_ref[...]
    cos_ref[...] = jnp.cos(ang)
    sin_ref[...] = jnp.sin(ang) * sgn_ref[...]


def _rope_tables(pos_col, inv_lane, sgn_lane):
    n = pos_col.shape[0]
    tile = 2048
    return pl.pallas_call(
        _rope_table_kernel,
        grid=(n // tile,),
        in_specs=[_row_spec(tile, 1), _const_spec((1, LANES)), _const_spec((1, LANES))],
        out_specs=[_row_spec(tile, LANES), _row_spec(tile, LANES)],
        out_shape=[jax.ShapeDtypeStruct((n, LANES), F32)] * 2,
        compiler_params=pltpu.CompilerParams(dimension_semantics=("parallel",)),
        name="rope_tables",
    )(pos_col, inv_lane, sgn_lane)


def _ffn_body(x, gpre, wg_ref, wu_ref, wd_ref, gpost, h_ref, acc_ref):
    h_ref[...] = _rms(x, gpre).astype(BF16)
    acc_ref[...] = jnp.zeros_like(acc_ref)

    def chunk(c, carry):
        h = h_ref[...]
        g = _dot(h, wg_ref[c])
        u = _dot(h, wu_ref[c])
        a = (g * jax.nn.sigmoid(g) * u).astype(BF16)
        acc_ref[...] += _dot(a, wd_ref[c])
        return carry

    lax.fori_loop(0, N_FF_CHUNKS, chunk, 0)
    return x + 0.5 * _rms(acc_ref[...], gpost)


def _ffn_kernel(x_ref, gpre_ref, wg_ref, wu_ref, wd_ref, gpost_ref, o_ref, h_ref, acc_ref):
    o_ref[...] = _ffn_body(x_ref[...], gpre_ref[...], wg_ref, wu_ref, wd_ref,
                           gpost_ref[...], h_ref, acc_ref)


def _ffn(x, gpre, wg, wu, wd, gpost):
    n = x.shape[0]
    tm = TOKEN_TILE
    return pl.pallas_call(
        _ffn_kernel,
        grid=(n // tm,),
        in_specs=[
            _row_spec(tm, D_MODEL),
            _const_spec((1, D_MODEL)),
            _const_spec((N_FF_CHUNKS, D_MODEL, FF_CHUNK)),
            _const_spec((N_FF_CHUNKS, D_MODEL, FF_CHUNK)),
            _const_spec((N_FF_CHUNKS, FF_CHUNK, D_MODEL)),
            _const_spec((1, D_MODEL)),
        ],
        out_specs=_row_spec(tm, D_MODEL),
        out_shape=jax.ShapeDtypeStruct((n, D_MODEL), F32),
        scratch_shapes=[pltpu.VMEM((tm, D_MODEL), BF16), pltpu.VMEM((tm, D_MODEL), F32)],
        compiler_params=pltpu.CompilerParams(
            dimension_semantics=("parallel",), vmem_limit_bytes=VMEM_LIMIT),
        name="ffn",
    )(x, gpre, wg, wu, wd, gpost)


def _swap_rope_halves(blk, lane):
    from_right = pltpu.roll(blk, LANES - MLA_ROPE // 2, axis=1)
    from_left = pltpu.roll(blk, MLA_ROPE // 2, axis=1)
    return jnp.where(lane < MLA_NOPE + MLA_ROPE // 2, from_right, from_left)


def _inproj_kernel(tiles_per_seq,
                   x_ref, g_ref, win_ref, gq_ref, wq_ref, gkv_ref, wkv_ref, wconv_ref,
                   cos_ref, sin_ref,
                   sbq_lo_ref, sbq_hi_ref, sbk_ref, sbv_lo_ref, sbv_hi_ref,
                   mq_ref, mk_ref, mv_lo_ref, mv_hi_ref, ycv_ref,
                   carry_ref):
    tm = x_ref.shape[0]
    h = _rms(x_ref[...], g_ref[...]).astype(BF16)
    cos = cos_ref[...]
    sin = sin_ref[...]
    lane = lax.broadcasted_iota(jnp.int32, (1, LANES), 1)

    def even_head_mask(width):
        col = lax.broadcasted_iota(jnp.int32, (1, width), 1)
        return (col % LANES) < HEAD_DIM

    sb = _dot(h, win_ref[:, _SB_Q:_C_Q])
    m_sb = even_head_mask(SB_WIDTH)
    q = sb[:, _SB_Q:_SB_K] * (HEAD_DIM ** -0.5)
    sbq_lo_ref[...] = jnp.where(m_sb, q, 0.0).astype(BF16)
    sbq_hi_ref[...] = jnp.where(m_sb, 0.0, q).astype(BF16)
    sbk_ref[...] = sb[:, _SB_K:_SB_V].astype(BF16)
    v = sb[:, _SB_V:_C_Q]
    sbv_lo_ref[...] = jnp.where(m_sb, v, 0.0).astype(BF16)
    sbv_hi_ref[...] = jnp.where(m_sb, 0.0, v).astype(BF16)

    cq = _dot(h, win_ref[:, _C_Q:_C_KV])
    q_raw = _dot(_rms(cq, gq_ref[...]).astype(BF16), wq_ref[...])
    q_scale = (MLA_NOPE + MLA_ROPE) ** -0.5
    for hd in range(MLA_HEADS):
        blk = q_raw[:, hd * LANES:(hd + 1) * LANES]
        roped = blk * cos + _swap_rope_halves(blk, lane) * sin
        mq_ref[:, hd * LANES:(hd + 1) * LANES] = (roped * q_scale).astype(BF16)

    ckv = _dot(h, win_ref[:, _C_KV:_CV])
    kv = _dot(_rms(ckv, gkv_ref[...]).astype(BF16), wkv_ref[...])
    kr = _dot(h, win_ref[:, _KR:IN_PACKED])
    kr = kr * cos + _swap_rope_halves(kr, lane) * sin
    for hd in range(MLA_HEADS):
        mk_ref[:, hd * LANES:(hd + 1) * LANES] = (
            kv[:, hd * LANES:(hd + 1) * LANES] + kr).astype(BF16)
    mv = kv[:, MLA_HEADS * LANES:]
    m_mv = even_head_mask(MLA_HEADS * HEAD_DIM)
    mv_lo_ref[...] = jnp.where(m_mv, mv, 0.0).astype(BF16)
    mv_hi_ref[...] = jnp.where(m_mv, 0.0, mv).astype(BF16)

    cv = _dot(h, win_ref[:, _CV:_KR])
    u = cv[:, CONV_WIDTH:2 * CONV_WIDTH] * cv[:, 2 * CONV_WIDTH:]
    seq_start = (pl.program_id(0) % tiles_per_seq) == 0
    prev = jnp.where(seq_start, 0.0, carry_ref[...])
    row = lax.broadcasted_iota(jnp.int32, (tm, CONV_WIDTH), 0)
    u1 = jnp.where(row == 0, prev[7:8, :], pltpu.roll(u, 1, axis=0))
    u2 = jnp.where(row == 0, prev[6:7, :],
                   jnp.where(row == 1, prev[7:8, :], pltpu.roll(u, 2, axis=0)))
    w = wconv_ref[...]
    y = cv[:, :CONV_WIDTH] * (w[2:3, :] * u + w[1:2, :] * u1 + w[0:1, :] * u2)
    ycv_ref[...] = y.astype(BF16)
    carry_ref[...] = u[tm - 8:, :]


def _inproj(x, g, win, gq, wq, gkv, wkv, wconv, cos_t, sin_t, seq_len):
    n = x.shape[0]
    tm = TOKEN_TILE
    out_widths = [SB_WIDTH] * 5 + [MLA_HEADS * LANES] * 2 + [MLA_HEADS * HEAD_DIM] * 2 + [CONV_WIDTH]
    return pl.pallas_call(
        functools.partial(_inproj_kernel, seq_len // tm),
        grid=(n // tm,),
        in_specs=[
            _row_spec(tm, D_MODEL),
            _const_spec((1, D_MODEL)),
            _const_spec((D_MODEL, IN_PACKED)),
            _const_spec((1, MLA_Q_RANK)),
            _const_spec((MLA_Q_RANK, MLA_HEADS * LANES)),
            _const_spec((1, MLA_KV_RANK)),
            _const_spec((MLA_KV_RANK, MLA_HEADS * (LANES + HEAD_DIM))),
            _const_spec((CONV_K, CONV_WIDTH)),
            _row_spec(tm, LANES),
            _row_spec(tm, LANES),
        ],
        out_specs=[_row_spec(tm, w) for w in out_widths],
        out_shape=[jax.ShapeDtypeStruct((n, w), BF16) for w in out_widths],
        scratch_shapes=[pltpu.VMEM((8, CONV_WIDTH), F32)],
        compiler_params=pltpu.CompilerParams(
            dimension_semantics=("arbitrary",), vmem_limit_bytes=VMEM_LIMIT),
        name="inproj",
    )(x, g, win, gq, wq, gkv, wkv, wconv, cos_t, sin_t)


def _sb_kernel(q_lo_ref, q_hi_ref, k_ref, v_lo_ref, v_hi_ref, o_ref):
    blk = ATT_BLOCK
    qi = pl.program_id(1)
    r = lax.broadcasted_iota(jnp.int32, (blk, blk), 0)
    c = lax.broadcasted_iota(jnp.int32, (blk, blk), 1)
    tri = jnp.where(r > c, 1.0, 0.0).astype(BF16)
    causal = c < r

    def block(qh, kh, vh, carry, acc, mask):
        z = _dot_nt(qh, kh)
        sp = jnp.maximum(z, 0.0) + jnp.log(1.0 + jnp.exp(-jnp.abs(z)))
        log_not = -sp if mask is None else jnp.where(mask, -sp, 0.0)
        ln_hi = log_not.astype(BF16)
        ln_lo = (log_not - ln_hi.astype(F32)).astype(BF16)
        after = _dot(ln_hi, tri) + _dot(ln_lo, tri) + carry
        w = jnp.exp(z - sp + after)
        if mask is not None:
            w = jnp.where(mask, w, 0.0)
        acc = acc + _dot(w.astype(BF16), vh)
        carry = carry + jnp.sum(log_not, axis=1, keepdims=True)
        return carry, acc

    for pair in range(SB_WIDTH // LANES):
        cols = slice(pair * LANES, (pair + 1) * LANES)
        acc = jnp.zeros((blk, LANES), F32)
        for q_ref, v_ref in ((q_lo_ref, v_lo_ref), (q_hi_ref, v_hi_ref)):
            qh = q_ref[0, :, cols]
            diag = pl.ds(pl.multiple_of(qi * blk, blk), blk)
            carry, acc = block(qh, k_ref[0, diag, cols], v_ref[0, diag, cols],
                               jnp.zeros((blk, 1), F32), acc, causal)

            def step(t, state, qh=qh, v_ref=v_ref):
                rows = pl.ds(pl.multiple_of((qi - 1 - t) * blk, blk), blk)
                return block(qh, k_ref[0, rows, cols], v_ref[0, rows, cols], *state, None)

            carry, acc = lax.fori_loop(0, qi, step, (carry, acc))
        o_ref[0, :, cols] = acc.astype(BF16)


def _sb_attention(q_lo, q_hi, k, v_lo, v_hi):
    b, s, w = k.shape
    q_spec = pl.BlockSpec((1, ATT_BLOCK, w), lambda bi, qi: (bi, qi, 0))
    kv_spec = pl.BlockSpec((1, s, w), lambda bi, qi: (bi, 0, 0))
    return pl.pallas_call(
        _sb_kernel,
        grid=(b, s // ATT_BLOCK),
        in_specs=[q_spec, q_spec, kv_spec, kv_spec, kv_spec],
        out_specs=q_spec,
        out_shape=jax.ShapeDtypeStruct((b, s, w), BF16),
        compiler_params=pltpu.CompilerParams(
            dimension_semantics=("parallel", "arbitrary"), vmem_limit_bytes=VMEM_LIMIT),
        name="sb_attention",
    )(q_lo, q_hi, k, v_lo, v_hi)


def _mla_kernel(q_ref, k_ref, v_lo_ref, v_hi_ref, o_ref):
    blk = ATT_BLOCK
    qi = pl.program_id(1)
    r = lax.broadcasted_iota(jnp.int32, (blk, blk), 0)
    c = lax.broadcasted_iota(jnp.int32, (blk, blk), 1)
    causal = c <= r

    for pair in range(MLA_HEADS // 2):
        vcols = slice(pair * LANES, (pair + 1) * LANES)
        out = jnp.zeros((blk, LANES), F32)
        for parity, v_ref in enumerate((v_lo_ref, v_hi_ref)):
            hd = 2 * pair + parity
            hcols = slice(hd * LANES, (hd + 1) * LANES)
            qh = q_ref[0, :, hcols]
            diag = pl.ds(pl.multiple_of(qi * blk, blk), blk)
            s = jnp.where(causal, _dot_nt(qh, k_ref[0, diag, hcols]), -1e30)
            m = jnp.max(s, axis=1, keepdims=True)
            p = jnp.exp(s - m)
            l = jnp.sum(p, axis=1, keepdims=True)
            acc = _dot(p.astype(BF16), v_ref[0, diag, vcols])

            def step(t, state, qh=qh, v_ref=v_ref, hcols=hcols, vcols=vcols):
                m, l, acc = state
                rows = pl.ds(pl.multiple_of(t * blk, blk), blk)
                s = _dot_nt(qh, k_ref[0, rows, hcols])
                m_new = jnp.maximum(m, jnp.max(s, axis=1, keepdims=True))
                alpha = jnp.exp(m - m_new)
                p = jnp.exp(s - m_new)
                l = alpha * l + jnp.sum(p, axis=1, keepdims=True)
                acc = alpha * acc + _dot(p.astype(BF16), v_ref[0, rows, vcols])
                return m_new, l, acc

            m, l, acc = lax.fori_loop(0, qi, step, (m, l, acc))
            out = out + acc / l
        o_ref[0, :, vcols] = out.astype(BF16)


def _mla_attention(q, k, v_lo, v_hi):
    b, s, wq = q.shape
    wv = v_lo.shape[-1]
    return pl.pallas_call(
        _mla_kernel,
        grid=(b, s // ATT_BLOCK),
        in_specs=[
            pl.BlockSpec((1, ATT_BLOCK, wq), lambda bi, qi: (bi, qi, 0)),
            pl.BlockSpec((1, s, wq), lambda bi, qi: (bi, 0, 0)),
            pl.BlockSpec((1, s, wv), lambda bi, qi: (bi, 0, 0)),
            pl.BlockSpec((1, s, wv), lambda bi, qi: (bi, 0, 0)),
        ],
        out_specs=pl.BlockSpec((1, ATT_BLOCK, wv), lambda bi, qi: (bi, qi, 0)),
        out_shape=jax.ShapeDtypeStruct((b, s, wv), BF16),
        compiler_params=pltpu.CompilerParams(
            dimension_semantics=("parallel", "arbitrary"), vmem_limit_bytes=VMEM_LIMIT),
        name="mla_attention",
    )(q, k, v_lo, v_hi)


def _outproj_kernel(x_ref, ysb_ref, ymla_ref, ycv_ref, wo_ref, g_ref, o_ref):
    sb_rows = SB_WIDTH
    mla_rows = SB_WIDTH + MLA_HEADS * HEAD_DIM
    mixed = (_dot(ysb_ref[...], wo_ref[:sb_rows, :])
             + _dot(ymla_ref[...], wo_ref[sb_rows:mla_rows, :])
             + _dot(ycv_ref[...], wo_ref[mla_rows:, :]))
    o_ref[...] = x_ref[...] + _rms(mixed, g_ref[...])


def _outproj(x, ysb, ymla, ycv, wo, g):
    n = x.shape[0]
    tm = TOKEN_TILE
    return pl.pallas_call(
        _outproj_kernel,
        grid=(n // tm,),
        in_specs=[
            _row_spec(tm, D_MODEL),
            _row_spec(tm, SB_WIDTH),
            _row_spec(tm, MLA_HEADS * HEAD_DIM),
            _row_spec(tm, CONV_WIDTH),
            _const_spec((D_MODEL, D_MODEL)),
            _const_spec((1, D_MODEL)),
        ],
        out_specs=_row_spec(tm, D_MODEL),
        out_shape=jax.ShapeDtypeStruct((n, D_MODEL), F32),
        compiler_params=pltpu.CompilerParams(
            dimension_semantics=("parallel",), vmem_limit_bytes=VMEM_LIMIT),
        name="outproj",
    )(x, ysb, ymla, ycv, wo, g)


def _ple_kernel(x_ref, p_ref, gpre_ref, wgate_ref, wproj_ref, gpost_ref, o_ref):
    x = x_ref[...]
    h = _rms(x, gpre_ref[...]).astype(BF16)
    e = jax.nn.sigmoid(_dot(h, wgate_ref[...])) * _dot(p_ref[...].astype(BF16), wproj_ref[...])
    o_ref[...] = x + _rms(e, gpost_ref[...])


def _ple(x, p, gpre, wgate, wproj, gpost):
    n = x.shape[0]
    tm = TOKEN_TILE
    return pl.pallas_call(
        _ple_kernel,
        grid=(n // tm,),
        in_specs=[
            _row_spec(tm, D_MODEL),
            _row_spec(tm, PLE_DIM),
            _const_spec((1, D_MODEL)),
            _const_spec((D_MODEL, D_MODEL)),
            _const_spec((PLE_DIM, D_MODEL)),
            _const_spec((1, D_MODEL)),
        ],
        out_specs=_row_spec(tm, D_MODEL),
        out_shape=jax.ShapeDtypeStruct((n, D_MODEL), F32),
        compiler_params=pltpu.CompilerParams(
            dimension_semantics=("parallel",), vmem_limit_bytes=VMEM_LIMIT),
        name="ple",
    )(x, p, gpre, wgate, wproj, gpost)


def _pack_ffn(w_gate, w_up, w_down):
    wg = w_gate.astype(BF16).reshape(D_MODEL, N_FF_CHUNKS, FF_CHUNK).transpose(1, 0, 2)
    wu = w_up.astype(BF16).reshape(D_MODEL, N_FF_CHUNKS, FF_CHUNK).transpose(1, 0, 2)
    wd = w_down.astype(BF16).reshape(N_FF_CHUNKS, FF_CHUNK, D_MODEL)
    return wg, wu, wd


def _pack_in(w_in):
    kr_at = _CV
    kr = w_in[:, kr_at:kr_at + MLA_ROPE]
    kr_tile = jnp.pad(kr, ((0, 0), (MLA_NOPE, LANES - MLA_NOPE - MLA_ROPE)))
    return jnp.concatenate([w_in[:, :kr_at], w_in[:, kr_at + MLA_ROPE:], kr_tile], axis=1).astype(BF16)


def _pack_uq(w_uq):
    per_head = MLA_NOPE + MLA_ROPE
    w = w_uq.reshape(MLA_Q_RANK, MLA_HEADS, per_head)
    w = jnp.pad(w, ((0, 0), (0, 0), (0, LANES - per_head)))
    return w.reshape(MLA_Q_RANK, MLA_HEADS * LANES).astype(BF16)


def _pack_ukv(w_ukv):
    w = w_ukv.reshape(MLA_KV_RANK, MLA_HEADS, MLA_NOPE + HEAD_DIM)
    wk = jnp.pad(w[:, :, :MLA_NOPE], ((0, 0), (0, 0), (0, LANES - MLA_NOPE)))
    wv = w[:, :, MLA_NOPE:]
    return jnp.concatenate([wk.reshape(MLA_KV_RANK, MLA_HEADS * LANES),
                            wv.reshape(MLA_KV_RANK, MLA_HEADS * HEAD_DIM)], axis=1).astype(BF16)


def kernel(x, p, positions, g_ffn1_pre, w1_gate, w1_up, w1_down, g_ffn1_post, g_mix_pre, w_in, g_mla_q, w_mla_uq, g_mla_kv, w_mla_ukv, w_conv, w_out, g_mix_post, g_ffn2_pre, w2_gate, w2_up, w2_down, g_ffn2_post, g_ple_pre, w_ple_gate, w_ple_proj, g_ple_post):
    batch, seq, _ = x.shape
    depth = w_in.shape[0]
    n = batch * seq
    assert seq % TOKEN_TILE == 0 and seq % ATT_BLOCK == 0

    half = MLA_ROPE // 2
    inv = ROPE_BASE ** (-jnp.arange(half, dtype=F32) / half)
    zeros = jnp.zeros
    inv_lane = jnp.concatenate([zeros(MLA_NOPE, F32), inv, inv, zeros(LANES - MLA_NOPE - MLA_ROPE, F32)])
    sgn_lane = jnp.concatenate([zeros(MLA_NOPE, F32), -jnp.ones(half, F32), jnp.ones(half, F32),
                                zeros(LANES - MLA_NOPE - MLA_ROPE, F32)])
    cos_t, sin_t = _rope_tables(positions.reshape(n, 1), inv_lane[None, :], sgn_lane[None, :])

    row = lambda g: g[None, :]
    xf = x.reshape(n, D_MODEL)
    for i in range(depth):
        xf = _ffn(xf, row(g_ffn1_pre[i]), *_pack_ffn(w1_gate[i], w1_up[i], w1_down[i]),
                  row(g_ffn1_post[i]))

        (sbq_lo, sbq_hi, sbk, sbv_lo, sbv_hi, mq, mk, mv_lo, mv_hi, ycv) = _inproj(
            xf, row(g_mix_pre[i]), _pack_in(w_in[i]), row(g_mla_q[i]), _pack_uq(w_mla_uq[i]),
            row(g_mla_kv[i]), _pack_ukv(w_mla_ukv[i]), w_conv[i], cos_t, sin_t, seq)
        seqs = lambda t: t.reshape(batch, seq, t.shape[-1])
        ysb = _sb_attention(seqs(sbq_lo), seqs(sbq_hi), seqs(sbk), seqs(sbv_lo), seqs(sbv_hi))
        ymla = _mla_attention(seqs(mq), seqs(mk), seqs(mv_lo), seqs(mv_hi))
        xf = _outproj(xf, ysb.reshape(n, -1), ymla.reshape(n, -1), ycv,
                      w_out[i].astype(BF16), row(g_mix_post[i]))

        xf = _ffn(xf, row(g_ffn2_pre[i]), *_pack_ffn(w2_gate[i], w2_up[i], w2_down[i]),
                  row(g_ffn2_post[i]))
        xf = _ple(xf, p[i].reshape(n, PLE_DIM), row(g_ple_pre[i]), w_ple_gate[i].astype(BF16),
                  w_ple_proj[i].astype(BF16), row(g_ple_post[i]))
    return xf.reshape(batch, seq, D_MODEL)
```

```python
import functools

import jax
import jax.numpy as jnp
from jax import lax
from jax.experimental import pallas as pl
from jax.experimental.pallas import tpu as pltpu

F32 = jnp.float32
BF16 = jnp.bfloat16

D_MODEL = 1024
D_FF = 2816
PLE_DIM = 256
HEAD_DIM = 64
SB_WIDTH = 256
MLA_HEADS = 8
MLA_NOPE = 64
MLA_ROPE = 32
MLA_Q_RANK = 384
MLA_KV_RANK = 256
CONV_WIDTH = 256
CONV_K = 3
ROPE_BASE = 10000.0
EPS = 1e-6
LOG2_E = 1.4426950408889634

LANES = 128
FF_CHUNK = 256
N_FF_CHUNKS = D_FF // FF_CHUNK
TOKEN_TILE = 512
ATT_BLOCK = 256
VMEM_LIMIT = 52 * 1024 * 1024
_SB_Q, _SB_K, _SB_V = 0, 256, 512
_C_Q = 768
_C_KV = _C_Q + MLA_Q_RANK
_CV = _C_KV + MLA_KV_RANK
_KR = _CV + 3 * CONV_WIDTH
IN_PACKED = _KR + LANES


def _rms(x, g):
    return x * lax.rsqrt(jnp.mean(x * x, axis=-1, keepdims=True) + EPS) * g


def _dot(a, b):
    return jnp.dot(a, b, preferred_element_type=F32)


def _dot_nt(a, b):
    return lax.dot_general(a, b, (((1,), (1,)), ((), ())), preferred_element_type=F32)


def _const_spec(shape):
    zeros = (0,) * len(shape)
    return pl.BlockSpec(shape, lambda *_: zeros, pipeline_mode=pl.Buffered(1))


def _row_spec(tile, width):
    return pl.BlockSpec((tile, width), lambda i: (i, 0))


def _rope_table_kernel(pos_ref, inv_ref, sgn_ref, cos_ref, sin_ref):
    ang = pos_ref[...].astype(F32) * inv_ref[...]
    cos_ref[...] = jnp.cos(ang)
    sin_ref[...] = jnp.sin(ang) * sgn_ref[...]


def _rope_tables(pos_col, inv_lane, sgn_lane):
    n = pos_col.shape[0]
    tile = 2048
    return pl.pallas_call(
        _rope_table_kernel,
        grid=(n // tile,),
        in_specs=[_row_spec(tile, 1), _const_spec((1, LANES)), _const_spec((1, LANES))],
        out_specs=[_row_spec(tile, LANES), _row_spec(tile, LANES)],
        out_shape=[jax.ShapeDtypeStruct((n, LANES), F32)] * 2,
        compiler_params=pltpu.CompilerParams(dimension_semantics=("parallel",)),
        name="rope_tables",
    )(pos_col, inv_lane, sgn_lane)


def _ffn_body(x, gpre, wg_ref, wu_ref, wd_ref, gpost, h_ref, acc_ref):
    h_ref[...] = _rms(x, gpre).astype(BF16)
    for c in range(N_FF_CHUNKS):
        cols = slice(c * FF_CHUNK, (c + 1) * FF_CHUNK)
        h = h_ref[...]
        g = _dot(h, wg_ref[:, cols])
        u = _dot(h, wu_ref[:, cols])
        a = (g * jax.nn.sigmoid(g) * u).astype(BF16)
        down = _dot(a, wd_ref[cols, :])
        if c == 0:
            acc_ref[...] = down
        else:
            acc_ref[...] += down
    return x + 0.5 * _rms(acc_ref[...], gpost)


_FFN_WEIGHT_SPECS = lambda: [
    _const_spec((1, D_MODEL)),
    _const_spec((D_MODEL, D_FF)),
    _const_spec((D_MODEL, D_FF)),
    _const_spec((D_FF, D_MODEL)),
    _const_spec((1, D_MODEL)),
]


def _swap_rope_halves(blk, lane):
    from_right = pltpu.roll(blk, LANES - MLA_ROPE // 2, axis=1)
    from_left = pltpu.roll(blk, MLA_ROPE // 2, axis=1)
    return jnp.where(lane < MLA_NOPE + MLA_ROPE // 2, from_right, from_left)


def _store_values_transposed(v, lo_ref, hi_ref):
    vt = v.T
    row = lax.broadcasted_iota(jnp.int32, (vt.shape[0], 1), 0)
    even = (row % LANES) < HEAD_DIM
    lo_ref[0] = jnp.where(even, vt, 0.0).astype(BF16)
    hi_ref[0] = jnp.where(even, 0.0, vt).astype(BF16)


def _premix_kernel(tiles_per_seq,
                   x_ref, gpre_ref, wg_ref, wu_ref, wd_ref, gpost_ref,
                   g_ref, win_ref, gq_ref, wq_ref, gkv_ref, wkv_ref, wconv_ref,
                   cos_ref, sin_ref,
                   xmid_ref, sbq_lo_ref, sbq_hi_ref, sbk_ref, sbv_lo_ref, sbv_hi_ref,
                   mq_ref, mk_ref, mv_lo_ref, mv_hi_ref, ycv_ref,
                   h_ref, acc_ref, carry_ref):
    tm = x_ref.shape[0]
    x = _ffn_body(x_ref[...], gpre_ref[...], wg_ref, wu_ref, wd_ref, gpost_ref[...], h_ref, acc_ref)
    xmid_ref[...] = x
    h = _rms(x, g_ref[...]).astype(BF16)
    cos = cos_ref[...]
    sin = sin_ref[...]
    lane = lax.broadcasted_iota(jnp.int32, (1, LANES), 1)

    def even_head_mask(width):
        col = lax.broadcasted_iota(jnp.int32, (1, width), 1)
        return (col % LANES) < HEAD_DIM

    sb = _dot(h, win_ref[:, _SB_Q:_C_Q])
    m_sb = even_head_mask(SB_WIDTH)
    q = sb[:, _SB_Q:_SB_K] * (HEAD_DIM ** -0.5 * LOG2_E)
    sbq_lo_ref[...] = jnp.where(m_sb, q, 0.0).astype(BF16)
    sbq_hi_ref[...] = jnp.where(m_sb, 0.0, q).astype(BF16)
    sbk_ref[...] = sb[:, _SB_K:_SB_V].astype(BF16)
    v = sb[:, _SB_V:_C_Q]
    sbv_lo_ref[...] = jnp.where(m_sb, v, 0.0).astype(BF16)
    sbv_hi_ref[...] = jnp.where(m_sb, 0.0, v).astype(BF16)

    cq = _dot(h, win_ref[:, _C_Q:_C_KV])
    q_raw = _dot(_rms(cq, gq_ref[...]).astype(BF16), wq_ref[...])
    q_scale = (MLA_NOPE + MLA_ROPE) ** -0.5 * LOG2_E
    for hd in range(MLA_HEADS):
        blk = q_raw[:, hd * LANES:(hd + 1) * LANES]
        roped = blk * cos + _swap_rope_halves(blk, lane) * sin
        mq_ref[:, hd * LANES:(hd + 1) * LANES] = (roped * q_scale).astype(BF16)

    ckv = _dot(h, win_ref[:, _C_KV:_CV])
    kv = _dot(_rms(ckv, gkv_ref[...]).astype(BF16), wkv_ref[...])
    kr = _dot(h, win_ref[:, _KR:IN_PACKED])
    kr = kr * cos + _swap_rope_halves(kr, lane) * sin
    for hd in range(MLA_HEADS):
        mk_ref[:, hd * LANES:(hd + 1) * LANES] = (
            kv[:, hd * LANES:(hd + 1) * LANES] + kr).astype(BF16)
    _store_values_transposed(kv[:, MLA_HEADS * LANES:], mv_lo_ref, mv_hi_ref)

    cv = _dot(h, win_ref[:, _CV:_KR])
    u = cv[:, CONV_WIDTH:2 * CONV_WIDTH] * cv[:, 2 * CONV_WIDTH:]
    seq_start = (pl.program_id(0) % tiles_per_seq) == 0
    prev = jnp.where(seq_start, 0.0, carry_ref[...])
    row = lax.broadcasted_iota(jnp.int32, (tm, CONV_WIDTH), 0)
    u1 = jnp.where(row == 0, prev[7:8, :], pltpu.roll(u, 1, axis=0))
    u2 = jnp.where(row == 0, prev[6:7, :],
                   jnp.where(row == 1, prev[7:8, :], pltpu.roll(u, 2, axis=0)))
    w = wconv_ref[...]
    y = cv[:, :CONV_WIDTH] * (w[2:3, :] * u + w[1:2, :] * u1 + w[0:1, :] * u2)
    ycv_ref[...] = y.astype(BF16)
    carry_ref[...] = u[tm - 8:, :]


def _premix(x, ffn_params, g, win, gq, wq, gkv, wkv, wconv, cos_t, sin_t, seq_len):
    n = x.shape[0]
    tm = TOKEN_TILE
    tiles_per_seq = seq_len // tm
    mla_v = MLA_HEADS * HEAD_DIM

    def rows(width):
        return _row_spec(tm, width), jax.ShapeDtypeStruct((n, width), BF16)

    def feature_major(width):
        spec = pl.BlockSpec((1, width, tm), lambda i: (i // tiles_per_seq, 0, i % tiles_per_seq))
        return spec, jax.ShapeDtypeStruct((n // seq_len, width, seq_len), BF16)

    outs = [(_row_spec(tm, D_MODEL), jax.ShapeDtypeStruct((n, D_MODEL), F32)),
            rows(SB_WIDTH), rows(SB_WIDTH), rows(SB_WIDTH), rows(SB_WIDTH), rows(SB_WIDTH),
            rows(MLA_HEADS * LANES), rows(MLA_HEADS * LANES),
            feature_major(mla_v), feature_major(mla_v), rows(CONV_WIDTH)]
    return pl.pallas_call(
        functools.partial(_premix_kernel, tiles_per_seq),
        grid=(n // tm,),
        in_specs=[
            _row_spec(tm, D_MODEL),
            *_FFN_WEIGHT_SPECS(),
            _const_spec((1, D_MODEL)),
            _const_spec((D_MODEL, IN_PACKED)),
            _const_spec((1, MLA_Q_RANK)),
            _const_spec((MLA_Q_RANK, MLA_HEADS * LANES)),
            _const_spec((1, MLA_KV_RANK)),
            _const_spec((MLA_KV_RANK, MLA_HEADS * (LANES + HEAD_DIM))),
            _const_spec((CONV_K, CONV_WIDTH)),
            _row_spec(tm, LANES),
            _row_spec(tm, LANES),
        ],
        out_specs=[spec for spec, _ in outs],
        out_shape=[shape for _, shape in outs],
        scratch_shapes=[pltpu.VMEM((tm, D_MODEL), BF16), pltpu.VMEM((tm, D_MODEL), F32),
                        pltpu.VMEM((8, CONV_WIDTH), F32)],
        compiler_params=pltpu.CompilerParams(
            dimension_semantics=("arbitrary",), vmem_limit_bytes=VMEM_LIMIT),
        name="premix",
    )(x, *ffn_params, g, win, gq, wq, gkv, wkv, wconv, cos_t, sin_t)


def _sb_kernel(q_lo_ref, q_hi_ref, k_ref, v_lo_ref, v_hi_ref, o_ref):
    blk = ATT_BLOCK
    qi = pl.program_id(1)
    r = lax.broadcasted_iota(jnp.int32, (blk, blk), 0)
    c = lax.broadcasted_iota(jnp.int32, (blk, blk), 1)
    neg_tri = jnp.where(r > c, -1.0, 0.0).astype(BF16)
    causal = c < r

    def weights(q_ref, rows, cols, carry, mask):
        z2 = _dot_nt(q_ref[0, :, cols], k_ref[0, rows, cols])
        sp2 = jnp.maximum(z2, 0.0) + jnp.log2(1.0 + jnp.exp2(-jnp.abs(z2)))
        sp2_in = sp2 if mask is None else jnp.where(mask, sp2, 0.0)
        hi = sp2_in.astype(BF16)
        lo = (sp2_in - hi.astype(F32)).astype(BF16)
        later = _dot(hi, neg_tri) + _dot(lo, neg_tri)
        w = jnp.exp2((z2 - sp2) + (later + carry))
        if mask is not None:
            w = jnp.where(mask, w, 0.0)
        return w.astype(BF16), carry + (later[:, 0:1] - sp2_in[:, 0:1])

    n_pairs = SB_WIDTH // LANES

    def all_heads(rows, state, mask):
        carries, accs = state
        new_carries, new_accs = [], []
        for pair in range(n_pairs):
            cols = slice(pair * LANES, (pair + 1) * LANES)
            w_lo, c_lo = weights(q_lo_ref, rows, cols, carries[2 * pair], mask)
            w_hi, c_hi = weights(q_hi_ref, rows, cols, carries[2 * pair + 1], mask)
            new_carries += [c_lo, c_hi]
            new_accs.append(accs[pair] + (_dot(w_lo, v_lo_ref[0, rows, cols])
                                          + _dot(w_hi, v_hi_ref[0, rows, cols])))
        return tuple(new_carries), tuple(new_accs)

    state = (tuple(jnp.zeros((blk, 1), F32) for _ in range(2 * n_pairs)),
             tuple(jnp.zeros((blk, LANES), F32) for _ in range(n_pairs)))
    state = all_heads(pl.ds(pl.multiple_of(qi * blk, blk), blk), state, causal)

    def step(t, state):
        return all_heads(pl.ds(pl.multiple_of((qi - 1 - t) * blk, blk), blk), state, None)

    _, accs = lax.fori_loop(0, qi, step, state)
    for pair in range(n_pairs):
        o_ref[0, :, pair * LANES:(pair + 1) * LANES] = accs[pair].astype(BF16)


def _sb_attention(q_lo, q_hi, k, v_lo, v_hi):
    b, s, w = k.shape
    q_spec = pl.BlockSpec((1, ATT_BLOCK, w), lambda bi, qi: (bi, qi, 0))
    kv_spec = pl.BlockSpec((1, s, w), lambda bi, qi: (bi, 0, 0))
    return pl.pallas_call(
        _sb_kernel,
        grid=(b, s // ATT_BLOCK),
        in_specs=[q_spec, q_spec, kv_spec, kv_spec, kv_spec],
        out_specs=q_spec,
        out_shape=jax.ShapeDtypeStruct((b, s, w), BF16),
        compiler_params=pltpu.CompilerParams(
            dimension_semantics=("parallel", "arbitrary"), vmem_limit_bytes=VMEM_LIMIT),
        name="sb_attention",
    )(q_lo, q_hi, k, v_lo, v_hi)


def _mla_kernel(q_ref, k_ref, vt_lo_ref, vt_hi_ref, o_ref):
    blk = ATT_BLOCK
    qi = pl.program_id(1)
    key = lax.broadcasted_iota(jnp.int32, (blk, blk), 0)
    qry = lax.broadcasted_iota(jnp.int32, (blk, blk), 1)
    causal = key <= qry

    n_pairs = MLA_HEADS // 2
    even_rows = lax.broadcasted_iota(jnp.int32, (LANES, 1), 0) < HEAD_DIM

    def probs(hd, rows, m, l, first):
        hcols = slice(hd * LANES, (hd + 1) * LANES)
        s = _dot_nt(k_ref[0, rows, hcols], q_ref[0, :, hcols])
        if first:
            s = jnp.where(causal, s, -1e30)
            m_new = jnp.max(s, axis=0, keepdims=True)
            p = jnp.exp2(s - m_new)
            return p.astype(BF16), m_new, jnp.sum(p, axis=0, keepdims=True), None
        m_new = jnp.maximum(m, jnp.max(s, axis=0, keepdims=True))
        alpha = jnp.exp2(m - m_new)
        p = jnp.exp2(s - m_new)
        return p.astype(BF16), m_new, alpha * l + jnp.sum(p, axis=0, keepdims=True), alpha

    def all_heads(rows, state, first):
        ms, ls, accs = state
        new_ms, new_ls, new_accs = [], [], []
        for pair in range(n_pairs):
            vrows = slice(pair * LANES, (pair + 1) * LANES)
            lo, hi = 2 * pair, 2 * pair + 1
            p_lo, m_lo, l_lo, a_lo = probs(lo, rows, ms[lo], ls[lo], first)
            p_hi, m_hi, l_hi, a_hi = probs(hi, rows, ms[hi], ls[hi], first)
            pv = _dot(vt_lo_ref[0, vrows, rows], p_lo) + _dot(vt_hi_ref[0, vrows, rows], p_hi)
            if not first:
                pv = accs[pair] * jnp.where(even_rows, a_lo, a_hi) + pv
            new_ms += [m_lo, m_hi]
            new_ls += [l_lo, l_hi]
            new_accs.append(pv)
        return tuple(new_ms), tuple(new_ls), tuple(new_accs)

    none = (None,) * MLA_HEADS
    state = all_heads(pl.ds(pl.multiple_of(qi * blk, blk), blk), (none, none, none[:n_pairs]), True)

    def step(t, state):
        return all_heads(pl.ds(pl.multiple_of(t * blk, blk), blk), state, False)

    _, ls, accs = lax.fori_loop(0, qi, step, state)
    for pair in range(n_pairs):
        denom = jnp.where(even_rows, ls[2 * pair], ls[2 * pair + 1])
        o_ref[0, :, pair * LANES:(pair + 1) * LANES] = (accs[pair] / denom).T.astype(BF16)


def _mla_attention(q, k, vt_lo, vt_hi):
    b, s, wq = q.shape
    wv = vt_lo.shape[1]
    return pl.pallas_call(
        _mla_kernel,
        grid=(b, s // ATT_BLOCK),
        in_specs=[
            pl.BlockSpec((1, ATT_BLOCK, wq), lambda bi, qi: (bi, qi, 0)),
            pl.BlockSpec((1, s, wq), lambda bi, qi: (bi, 0, 0)),
            pl.BlockSpec((1, wv, s), lambda bi, qi: (bi, 0, 0)),
            pl.BlockSpec((1, wv, s), lambda bi, qi: (bi, 0, 0)),
        ],
        out_specs=pl.BlockSpec((1, ATT_BLOCK, wv), lambda bi, qi: (bi, qi, 0)),
        out_shape=jax.ShapeDtypeStruct((b, s, wv), BF16),
        compiler_params=pltpu.CompilerParams(
            dimension_semantics=("parallel", "arbitrary"), vmem_limit_bytes=VMEM_LIMIT),
        name="mla_attention",
    )(q, k, vt_lo, vt_hi)


def _postmix_kernel(x_ref, ysb_ref, ymla_ref, ycv_ref, p_ref, wo_ref, gmix_ref,
                    gpre_ref, wg_ref, wu_ref, wd_ref, gpost_ref,
                    gple_pre_ref, wgate_ref, wproj_ref, gple_post_ref,
                    o_ref, h_ref, acc_ref):
    sb_rows = SB_WIDTH
    mla_rows = SB_WIDTH + MLA_HEADS * HEAD_DIM
    mixed = (_dot(ysb_ref[...], wo_ref[:sb_rows, :])
             + _dot(ymla_ref[...], wo_ref[sb_rows:mla_rows, :])
             + _dot(ycv_ref[...], wo_ref[mla_rows:, :]))
    x = x_ref[...] + _rms(mixed, gmix_ref[...])
    x = _ffn_body(x, gpre_ref[...], wg_ref, wu_ref, wd_ref, gpost_ref[...], h_ref, acc_ref)
    h = _rms(x, gple_pre_ref[...]).astype(BF16)
    e = jax.nn.sigmoid(_dot(h, wgate_ref[...])) * _dot(p_ref[...].astype(BF16), wproj_ref[...])
    o_ref[...] = x + _rms(e, gple_post_ref[...])


def _postmix(x, ysb, ymla, ycv, p, wo, gmix, ffn_params, gple_pre, wgate, wproj, gple_post):
    n = x.shape[0]
    tm = TOKEN_TILE
    return pl.pallas_call(
        _postmix_kernel,
        grid=(n // tm,),
        in_specs=[
            _row_spec(tm, D_MODEL),
            _row_spec(tm, SB_WIDTH),
            _row_spec(tm, MLA_HEADS * HEAD_DIM),
            _row_spec(tm, CONV_WIDTH),
            _row_spec(tm, PLE_DIM),
            _const_spec((D_MODEL, D_MODEL)),
            _const_spec((1, D_MODEL)),
            *_FFN_WEIGHT_SPECS(),
            _const_spec((1, D_MODEL)),
            _const_spec((D_MODEL, D_MODEL)),
            _const_spec((PLE_DIM, D_MODEL)),
            _const_spec((1, D_MODEL)),
        ],
        out_specs=_row_spec(tm, D_MODEL),
        out_shape=jax.ShapeDtypeStruct((n, D_MODEL), F32),
        scratch_shapes=[pltpu.VMEM((tm, D_MODEL), BF16), pltpu.VMEM((tm, D_MODEL), F32)],
        compiler_params=pltpu.CompilerParams(
            dimension_semantics=("parallel",), vmem_limit_bytes=VMEM_LIMIT),
        name="postmix",
    )(x, ysb, ymla, ycv, p, wo, gmix, *ffn_params, gple_pre, wgate, wproj, gple_post)


def _ffn_params(g_pre, w_gate, w_up, w_down, g_post):
    return (g_pre[None, :], w_gate.astype(BF16), w_up.astype(BF16), w_down.astype(BF16), g_post[None, :])


def _pack_in(w_in):
    kr_at = _CV
    kr = w_in[:, kr_at:kr_at + MLA_ROPE]
    kr_tile = jnp.pad(kr, ((0, 0), (MLA_NOPE, LANES - MLA_NOPE - MLA_ROPE)))
    return jnp.concatenate([w_in[:, :kr_at], w_in[:, kr_at + MLA_ROPE:], kr_tile], axis=1).astype(BF16)


def _pack_uq(w_uq):
    per_head = MLA_NOPE + MLA_ROPE
    w = w_uq.reshape(MLA_Q_RANK, MLA_HEADS, per_head)
    w = jnp.pad(w, ((0, 0), (0, 0), (0, LANES - per_head)))
    return w.reshape(MLA_Q_RANK, MLA_HEADS * LANES).astype(BF16)


def _pack_ukv(w_ukv):
    w = w_ukv.reshape(MLA_KV_RANK, MLA_HEADS, MLA_NOPE + HEAD_DIM)
    wk = jnp.pad(w[:, :, :MLA_NOPE], ((0, 0), (0, 0), (0, LANES - MLA_NOPE)))
    wv = w[:, :, MLA_NOPE:]
    return jnp.concatenate([wk.reshape(MLA_KV_RANK, MLA_HEADS * LANES),
                            wv.reshape(MLA_KV_RANK, MLA_HEADS * HEAD_DIM)], axis=1).astype(BF16)


def kernel(x, p, positions, g_ffn1_pre, w1_gate, w1_up, w1_down, g_ffn1_post, g_mix_pre, w_in, g_mla_q, w_mla_uq, g_mla_kv, w_mla_ukv, w_conv, w_out, g_mix_post, g_ffn2_pre, w2_gate, w2_up, w2_down, g_ffn2_post, g_ple_pre, w_ple_gate, w_ple_proj, g_ple_post):
    batch, seq, _ = x.shape
    depth = w_in.shape[0]
    n = batch * seq
    assert seq % TOKEN_TILE == 0 and seq % ATT_BLOCK == 0

    half = MLA_ROPE // 2
    inv = ROPE_BASE ** (-jnp.arange(half, dtype=F32) / half)
    zeros = jnp.zeros
    inv_lane = jnp.concatenate([zeros(MLA_NOPE, F32), inv, inv, zeros(LANES - MLA_NOPE - MLA_ROPE, F32)])
    sgn_lane = jnp.concatenate([zeros(MLA_NOPE, F32), -jnp.ones(half, F32), jnp.ones(half, F32),
                                zeros(LANES - MLA_NOPE - MLA_ROPE, F32)])
    cos_t, sin_t = _rope_tables(positions.reshape(n, 1), inv_lane[None, :], sgn_lane[None, :])

    row = lambda g: g[None, :]
    xf = x.reshape(n, D_MODEL)
    for i in range(depth):
        (xf, sbq_lo, sbq_hi, sbk, sbv_lo, sbv_hi, mq, mk, mvt_lo, mvt_hi, ycv) = _premix(
            xf, _ffn_params(g_ffn1_pre[i], w1_gate[i], w1_up[i], w1_down[i], g_ffn1_post[i]),
            row(g_mix_pre[i]), _pack_in(w_in[i]), row(g_mla_q[i]), _pack_uq(w_mla_uq[i]),
            row(g_mla_kv[i]), _pack_ukv(w_mla_ukv[i]), w_conv[i], cos_t, sin_t, seq)
        seqs = lambda t: t.reshape(batch, seq, t.shape[-1])
        ysb = _sb_attention(seqs(sbq_lo), seqs(sbq_hi), seqs(sbk), seqs(sbv_lo), seqs(sbv_hi))
        ymla = _mla_attention(seqs(mq), seqs(mk), mvt_lo, mvt_hi)
        xf = _postmix(
            xf, ysb.reshape(n, -1), ymla.reshape(n, -1), ycv, p[i].reshape(n, PLE_DIM),
            w_out[i].astype(BF16), row(g_mix_post[i]),
            _ffn_params(g_ffn2_pre[i], w2_gate[i], w2_up[i], w2_down[i], g_ffn2_post[i]),
            row(g_ple_pre[i]), w_ple_gate[i].astype(BF16), w_ple_proj[i].astype(BF16),
            row(g_ple_post[i]))
    return xf.reshape(batch, seq, D_MODEL)
```

```python
import functools

import jax
import jax.numpy as jnp
from jax import lax
from jax.experimental import pallas as pl
from jax.experimental.pallas import tpu as pltpu

F32 = jnp.float32
BF16 = jnp.bfloat16

D_MODEL = 1024
D_FF = 2816
PLE_DIM = 256
HEAD_DIM = 64
SB_WIDTH = 256
MLA_HEADS = 8
MLA_NOPE = 64
MLA_ROPE = 32
MLA_Q_RANK = 384
MLA_KV_RANK = 256
CONV_WIDTH = 256
CONV_K = 3
ROPE_BASE = 10000.0
EPS = 1e-6
LOG2_E = 1.4426950408889634

LANES = 128
FF_CHUNK = 256
N_FF_CHUNKS = D_FF // FF_CHUNK
TOKEN_TILE = 512
ATT_BLOCK = 256
SB_Q_BLOCK = 1024
MLA_Q_BLOCK = 1024
VMEM_LIMIT = 52 * 1024 * 1024
_SB_Q, _SB_K, _SB_V = 0, 256, 512
_C_Q = 768
_C_KV = _C_Q + MLA_Q_RANK
_CV = _C_KV + MLA_KV_RANK
_KR = _CV + 3 * CONV_WIDTH
IN_PACKED = _KR + LANES


def _rms(x, g):
    return x * lax.rsqrt(jnp.mean(x * x, axis=-1, keepdims=True) + EPS) * g


def _dot(a, b):
    return jnp.dot(a, b, preferred_element_type=F32)


def _dot_nt(a, b):
    return lax.dot_general(a, b, (((1,), (1,)), ((), ())), preferred_element_type=F32)


def _const_spec(shape):
    zeros = (0,) * len(shape)
    return pl.BlockSpec(shape, lambda *_: zeros, pipeline_mode=pl.Buffered(1))


def _layer_spec(layer, shape):
    index = (layer,) + (0,) * len(shape)
    return pl.BlockSpec((None, *shape), lambda *_: index, pipeline_mode=pl.Buffered(1))


def _row_spec(tile, width):
    return pl.BlockSpec((tile, width), lambda i: (i, 0))


def _rope_table_kernel(pos_ref, inv_ref, sgn_ref, cos_ref, sin_ref):
    ang = pos_ref[...].astype(F32) * inv_ref[...]
    cos_ref[...] = jnp.cos(ang)
    sin_ref[...] = jnp.sin(ang) * sgn_ref[...]


def _rope_tables(pos_col, inv_lane, sgn_lane):
    n = pos_col.shape[0]
    tile = 2048
    return pl.pallas_call(
        _rope_table_kernel,
        grid=(n // tile,),
        in_specs=[_row_spec(tile, 1), _const_spec((1, LANES)), _const_spec((1, LANES))],
        out_specs=[_row_spec(tile, LANES), _row_spec(tile, LANES)],
        out_shape=[jax.ShapeDtypeStruct((n, LANES), F32)] * 2,
        compiler_params=pltpu.CompilerParams(dimension_semantics=("parallel",)),
        name="rope_tables",
    )(pos_col, inv_lane, sgn_lane)


def _ffn_body(x, gpre, wg_ref, wu_ref, wd_ref, gpost, h_ref, acc_ref):
    h_ref[...] = _rms(x, gpre).astype(BF16)
    for c in range(N_FF_CHUNKS):
        cols = slice(c * FF_CHUNK, (c + 1) * FF_CHUNK)
        h = h_ref[...]
        g = _dot(h, wg_ref[:, cols])
        u = _dot(h, wu_ref[:, cols])
        a = (g * jax.nn.sigmoid(g) * u).astype(BF16)
        down = _dot(a, wd_ref[cols, :])
        if c == 0:
            acc_ref[...] = down
        else:
            acc_ref[...] += down
    return x + 0.5 * _rms(acc_ref[...], gpost)


def _ffn_weight_specs(layer):
    return [
        _layer_spec(layer, (1, D_MODEL)),
        _layer_spec(layer, (D_MODEL, D_FF)),
        _layer_spec(layer, (D_MODEL, D_FF)),
        _layer_spec(layer, (D_FF, D_MODEL)),
        _layer_spec(layer, (1, D_MODEL)),
    ]


def _swap_rope_halves(blk, lane):
    from_right = pltpu.roll(blk, LANES - MLA_ROPE // 2, axis=1)
    from_left = pltpu.roll(blk, MLA_ROPE // 2, axis=1)
    return jnp.where(lane < MLA_NOPE + MLA_ROPE // 2, from_right, from_left)


def _store_values_transposed(v, lo_ref, hi_ref):
    vt = v.T
    row = lax.broadcasted_iota(jnp.int32, (vt.shape[0], 1), 0)
    even = (row % LANES) < HEAD_DIM
    lo_ref[0] = jnp.where(even, vt, 0.0).astype(BF16)
    hi_ref[0] = jnp.where(even, 0.0, vt).astype(BF16)


def _premix_kernel(tiles_per_seq,
                   x_ref, gpre_ref, wg_ref, wu_ref, wd_ref, gpost_ref,
                   g_ref, win_ref, gq_ref, wq_ref, gkv_ref, wkv_ref, wconv_ref,
                   cos_ref, sin_ref,
                   xmid_ref, sbq_lo_ref, sbq_hi_ref, sbk_ref, sbv_lo_ref, sbv_hi_ref,
                   mq_ref, mk_ref, mv_lo_ref, mv_hi_ref, ycv_ref,
                   h_ref, acc_ref, carry_ref):
    tm = x_ref.shape[0]
    x = _ffn_body(x_ref[...], gpre_ref[...], wg_ref, wu_ref, wd_ref, gpost_ref[...], h_ref, acc_ref)
    xmid_ref[...] = x
    h = _rms(x, g_ref[...]).astype(BF16)
    cos = cos_ref[...]
    sin = sin_ref[...]
    lane = lax.broadcasted_iota(jnp.int32, (1, LANES), 1)

    def even_head_mask(width):
        col = lax.broadcasted_iota(jnp.int32, (1, width), 1)
        return (col % LANES) < HEAD_DIM

    sb = _dot(h, win_ref[:, _SB_Q:_C_Q])
    m_sb = even_head_mask(SB_WIDTH)
    q = sb[:, _SB_Q:_SB_K] * (HEAD_DIM ** -0.5 * LOG2_E)
    sbq_lo_ref[...] = jnp.where(m_sb, q, 0.0).astype(BF16)
    sbq_hi_ref[...] = jnp.where(m_sb, 0.0, q).astype(BF16)
    sbk_ref[...] = sb[:, _SB_K:_SB_V].astype(BF16)
    v = sb[:, _SB_V:_C_Q]
    sbv_lo_ref[...] = jnp.where(m_sb, v, 0.0).astype(BF16)
    sbv_hi_ref[...] = jnp.where(m_sb, 0.0, v).astype(BF16)

    cq = _dot(h, win_ref[:, _C_Q:_C_KV])
    q_raw = _dot(_rms(cq, gq_ref[...]).astype(BF16), wq_ref[...])
    q_scale = (MLA_NOPE + MLA_ROPE) ** -0.5 * LOG2_E
    for hd in range(MLA_HEADS):
        blk = q_raw[:, hd * LANES:(hd + 1) * LANES]
        roped = blk * cos + _swap_rope_halves(blk, lane) * sin
        mq_ref[:, hd * LANES:(hd + 1) * LANES] = (roped * q_scale).astype(BF16)

    ckv = _dot(h, win_ref[:, _C_KV:_CV])
    kv = _dot(_rms(ckv, gkv_ref[...]).astype(BF16), wkv_ref[...])
    kr = _dot(h, win_ref[:, _KR:IN_PACKED])
    kr = kr * cos + _swap_rope_halves(kr, lane) * sin
    for hd in range(MLA_HEADS):
        mk_ref[:, hd * LANES:(hd + 1) * LANES] = (
            kv[:, hd * LANES:(hd + 1) * LANES] + kr).astype(BF16)
    _store_values_transposed(kv[:, MLA_HEADS * LANES:], mv_lo_ref, mv_hi_ref)

    cv = _dot(h, win_ref[:, _CV:_KR])
    u = cv[:, CONV_WIDTH:2 * CONV_WIDTH] * cv[:, 2 * CONV_WIDTH:]
    seq_start = (pl.program_id(0) % tiles_per_seq) == 0
    prev = jnp.where(seq_start, 0.0, carry_ref[...])
    row = lax.broadcasted_iota(jnp.int32, (tm, CONV_WIDTH), 0)
    u1 = jnp.where(row == 0, prev[7:8, :], pltpu.roll(u, 1, axis=0))
    u2 = jnp.where(row == 0, prev[6:7, :],
                   jnp.where(row == 1, prev[7:8, :], pltpu.roll(u, 2, axis=0)))
    w = wconv_ref[...]
    y = cv[:, :CONV_WIDTH] * (w[2:3, :] * u + w[1:2, :] * u1 + w[0:1, :] * u2)
    ycv_ref[...] = y.astype(BF16)
    carry_ref[...] = u[tm - 8:, :]


def _premix(x, layer, ffn_params, g, win, gq, wq, gkv, wkv, wconv, cos_t, sin_t, seq_len):
    n = x.shape[0]
    tm = TOKEN_TILE
    tiles_per_seq = seq_len // tm
    mla_v = MLA_HEADS * HEAD_DIM

    def rows(width):
        return _row_spec(tm, width), jax.ShapeDtypeStruct((n, width), BF16)

    def feature_major(width):
        spec = pl.BlockSpec((1, width, tm), lambda i: (i // tiles_per_seq, 0, i % tiles_per_seq))
        return spec, jax.ShapeDtypeStruct((n // seq_len, width, seq_len), BF16)

    outs = [(_row_spec(tm, D_MODEL), jax.ShapeDtypeStruct((n, D_MODEL), F32)),
            rows(SB_WIDTH), rows(SB_WIDTH), rows(SB_WIDTH), rows(SB_WIDTH), rows(SB_WIDTH),
            rows(MLA_HEADS * LANES), rows(MLA_HEADS * LANES),
            feature_major(mla_v), feature_major(mla_v), rows(CONV_WIDTH)]
    return pl.pallas_call(
        functools.partial(_premix_kernel, tiles_per_seq),
        grid=(n // tm,),
        in_specs=[
            _row_spec(tm, D_MODEL),
            *_ffn_weight_specs(layer),
            _layer_spec(layer, (1, D_MODEL)),
            _layer_spec(layer, (D_MODEL, IN_PACKED)),
            _layer_spec(layer, (1, MLA_Q_RANK)),
            _layer_spec(layer, (MLA_Q_RANK, MLA_HEADS * LANES)),
            _layer_spec(layer, (1, MLA_KV_RANK)),
            _layer_spec(layer, (MLA_KV_RANK, MLA_HEADS * (LANES + HEAD_DIM))),
            _layer_spec(layer, (CONV_K, CONV_WIDTH)),
            _row_spec(tm, LANES),
            _row_spec(tm, LANES),
        ],
        out_specs=[spec for spec, _ in outs],
        out_shape=[shape for _, shape in outs],
        scratch_shapes=[pltpu.VMEM((tm, D_MODEL), BF16), pltpu.VMEM((tm, D_MODEL), F32),
                        pltpu.VMEM((8, CONV_WIDTH), F32)],
        compiler_params=pltpu.CompilerParams(
            dimension_semantics=("arbitrary",), vmem_limit_bytes=VMEM_LIMIT),
        name="premix",
    )(x, *ffn_params, g, win, gq, wq, gkv, wkv, wconv, cos_t, sin_t)


def _sb_kernel(q_lo_ref, q_hi_ref, k_ref, v_lo_ref, v_hi_ref, o_ref):
    blk = ATT_BLOCK
    tq = q_lo_ref.shape[1]
    diag_blocks = tq // blk
    qi = pl.program_id(1)
    r = lax.broadcasted_iota(jnp.int32, (blk, blk), 0)
    c = lax.broadcasted_iota(jnp.int32, (blk, blk), 1)
    neg_tri = jnp.where(r > c, -1.0, 0.0).astype(BF16)

    def rows_from(q0, old, new):
        return new if q0 == 0 else jnp.concatenate([old[:q0], new], axis=0)

    def weights(q_ref, rows, cols, carry, q0, masked):
        z2 = _dot_nt(q_ref[0, q0:, cols], k_ref[0, rows, cols])
        sp2 = jnp.maximum(z2, 0.0) + jnp.log2(1.0 + jnp.exp2(-jnp.abs(z2)))
        if masked:
            qry = lax.broadcasted_iota(jnp.int32, z2.shape, 0)
            key = lax.broadcasted_iota(jnp.int32, z2.shape, 1)
            visible = key < qry
            sp2_in = jnp.where(visible, sp2, 0.0)
        else:
            sp2_in = sp2
        hi = sp2_in.astype(BF16)
        lo = (sp2_in - hi.astype(F32)).astype(BF16)
        later = _dot(hi, neg_tri) + _dot(lo, neg_tri)
        w = jnp.exp2((z2 - sp2) + (later + carry[q0:]))
        if masked:
            w = jnp.where(visible, w, 0.0)
        new_carry = carry[q0:] + (later[:, 0:1] - sp2_in[:, 0:1])
        return w.astype(BF16), rows_from(q0, carry, new_carry)

    n_pairs = SB_WIDTH // LANES

    def all_heads(rows, state, q0=0, masked=False):
        carries, accs = state
        new_carries, new_accs = [], []
        for pair in range(n_pairs):
            cols = slice(pair * LANES, (pair + 1) * LANES)
            w_lo, c_lo = weights(q_lo_ref, rows, cols, carries[2 * pair], q0, masked)
            w_hi, c_hi = weights(q_hi_ref, rows, cols, carries[2 * pair + 1], q0, masked)
            new_carries += [c_lo, c_hi]
            acc = accs[pair][q0:] + (_dot(w_lo, v_lo_ref[0, rows, cols])
                                     + _dot(w_hi, v_hi_ref[0, rows, cols]))
            new_accs.append(rows_from(q0, accs[pair], acc))
        return tuple(new_carries), tuple(new_accs)

    state = (tuple(jnp.zeros((tq, 1), F32) for _ in range(2 * n_pairs)),
             tuple(jnp.zeros((tq, LANES), F32) for _ in range(n_pairs)))
    for d in reversed(range(diag_blocks)):
        rows = pl.ds(pl.multiple_of((qi * diag_blocks + d) * blk, blk), blk)
        state = all_heads(rows, state, q0=d * blk, masked=True)

    def step(t, state):
        return all_heads(pl.ds(pl.multiple_of((qi * diag_blocks - 1 - t) * blk, blk), blk), state)

    _, accs = lax.fori_loop(0, qi * diag_blocks, step, state)
    for pair in range(n_pairs):
        o_ref[0, :, pair * LANES:(pair + 1) * LANES] = accs[pair].astype(BF16)


def _sb_attention(q_lo, q_hi, k, v_lo, v_hi):
    b, s, w = k.shape
    q_spec = pl.BlockSpec((1, SB_Q_BLOCK, w), lambda bi, qi: (bi, qi, 0))
    kv_spec = pl.BlockSpec((1, s, w), lambda bi, qi: (bi, 0, 0))
    return pl.pallas_call(
        _sb_kernel,
        grid=(b, s // SB_Q_BLOCK),
        in_specs=[q_spec, q_spec, kv_spec, kv_spec, kv_spec],
        out_specs=q_spec,
        out_shape=jax.ShapeDtypeStruct((b, s, w), BF16),
        compiler_params=pltpu.CompilerParams(
            dimension_semantics=("parallel", "arbitrary"), vmem_limit_bytes=VMEM_LIMIT),
        name="sb_attention",
    )(q_lo, q_hi, k, v_lo, v_hi)


def _mla_kernel(q_ref, k_ref, vt_lo_ref, vt_hi_ref, o_ref):
    blk = ATT_BLOCK
    tq = q_ref.shape[1]
    diag_blocks = tq // blk
    qi = pl.program_id(1)
    n_pairs = MLA_HEADS // 2
    even_rows = lax.broadcasted_iota(jnp.int32, (LANES, 1), 0) < HEAD_DIM

    def lanes_from(q0, old, new):
        return new if q0 == 0 else jnp.concatenate([old[:, :q0], new], axis=1)

    def probs(hd, rows, m, l, q0, masked):
        hcols = slice(hd * LANES, (hd + 1) * LANES)
        s = _dot_nt(k_ref[0, rows, hcols], q_ref[0, q0:, hcols])
        if masked:
            key = lax.broadcasted_iota(jnp.int32, s.shape, 0)
            qry = lax.broadcasted_iota(jnp.int32, s.shape, 1)
            s = jnp.where(key <= qry, s, -1e30)
        if m is None:
            m_new = jnp.max(s, axis=0, keepdims=True)
            p = jnp.exp2(s - m_new)
            return p.astype(BF16), m_new, jnp.sum(p, axis=0, keepdims=True), None
        m_new = jnp.maximum(m[:, q0:], jnp.max(s, axis=0, keepdims=True))
        alpha = jnp.exp2(m[:, q0:] - m_new)
        p = jnp.exp2(s - m_new)
        l_new = alpha * l[:, q0:] + jnp.sum(p, axis=0, keepdims=True)
        return p.astype(BF16), lanes_from(q0, m, m_new), lanes_from(q0, l, l_new), alpha

    def all_heads(rows, state, q0=0, masked=False):
        ms, ls, accs = state
        new_ms, new_ls, new_accs = [], [], []
        for pair in range(n_pairs):
            vrows = slice(pair * LANES, (pair + 1) * LANES)
            lo, hi = 2 * pair, 2 * pair + 1
            p_lo, m_lo, l_lo, a_lo = probs(lo, rows, ms[lo], ls[lo], q0, masked)
            p_hi, m_hi, l_hi, a_hi = probs(hi, rows, ms[hi], ls[hi], q0, masked)
            pv = _dot(vt_lo_ref[0, vrows, rows], p_lo) + _dot(vt_hi_ref[0, vrows, rows], p_hi)
            if accs[pair] is not None:
                pv = lanes_from(q0, accs[pair],
                                accs[pair][:, q0:] * jnp.where(even_rows, a_lo, a_hi) + pv)
            new_ms += [m_lo, m_hi]
            new_ls += [l_lo, l_hi]
            new_accs.append(pv)
        return tuple(new_ms), tuple(new_ls), tuple(new_accs)

    none = (None,) * MLA_HEADS
    state = (none, none, none[:n_pairs])
    for d in range(diag_blocks):
        rows = pl.ds(pl.multiple_of((qi * diag_blocks + d) * blk, blk), blk)
        state = all_heads(rows, state, q0=d * blk, masked=True)

    def step(t, state):
        return all_heads(pl.ds(pl.multiple_of(t * blk, blk), blk), state)

    _, ls, accs = lax.fori_loop(0, qi * diag_blocks, step, state)
    for pair in range(n_pairs):
        denom = jnp.where(even_rows, ls[2 * pair], ls[2 * pair + 1])
        o_ref[0, :, pair * LANES:(pair + 1) * LANES] = (accs[pair] / denom).T.astype(BF16)


def _mla_attention(q, k, vt_lo, vt_hi):
    b, s, wq = q.shape
    wv = vt_lo.shape[1]
    return pl.pallas_call(
        _mla_kernel,
        grid=(b, s // MLA_Q_BLOCK),
        in_specs=[
            pl.BlockSpec((1, MLA_Q_BLOCK, wq), lambda bi, qi: (bi, qi, 0)),
            pl.BlockSpec((1, s, wq), lambda bi, qi: (bi, 0, 0)),
            pl.BlockSpec((1, wv, s), lambda bi, qi: (bi, 0, 0)),
            pl.BlockSpec((1, wv, s), lambda bi, qi: (bi, 0, 0)),
        ],
        out_specs=pl.BlockSpec((1, MLA_Q_BLOCK, wv), lambda bi, qi: (bi, qi, 0)),
        out_shape=jax.ShapeDtypeStruct((b, s, wv), BF16),
        compiler_params=pltpu.CompilerParams(
            dimension_semantics=("parallel", "arbitrary"), vmem_limit_bytes=VMEM_LIMIT),
        name="mla_attention",
    )(q, k, vt_lo, vt_hi)


def _postmix_kernel(x_ref, ysb_ref, ymla_ref, ycv_ref, p_ref, wo_ref, gmix_ref,
                    gpre_ref, wg_ref, wu_ref, wd_ref, gpost_ref,
                    gple_pre_ref, wgate_ref, wproj_ref, gple_post_ref,
                    o_ref, h_ref, acc_ref):
    sb_rows = SB_WIDTH
    mla_rows = SB_WIDTH + MLA_HEADS * HEAD_DIM
    mixed = (_dot(ysb_ref[...], wo_ref[:sb_rows, :])
             + _dot(ymla_ref[...], wo_ref[sb_rows:mla_rows, :])
             + _dot(ycv_ref[...], wo_ref[mla_rows:, :]))
    x = x_ref[...] + _rms(mixed, gmix_ref[...])
    x = _ffn_body(x, gpre_ref[...], wg_ref, wu_ref, wd_ref, gpost_ref[...], h_ref, acc_ref)
    h = _rms(x, gple_pre_ref[...]).astype(BF16)
    e = jax.nn.sigmoid(_dot(h, wgate_ref[...])) * _dot(p_ref[...].astype(BF16), wproj_ref[...])
    o_ref[...] = x + _rms(e, gple_post_ref[...])


def _postmix(x, layer, ysb, ymla, ycv, p, wo, gmix, ffn_params, gple_pre, wgate, wproj, gple_post):
    n = x.shape[0]
    tm = TOKEN_TILE
    return pl.pallas_call(
        _postmix_kernel,
        grid=(n // tm,),
        in_specs=[
            _row_spec(tm, D_MODEL),
            _row_spec(tm, SB_WIDTH),
            _row_spec(tm, MLA_HEADS * HEAD_DIM),
            _row_spec(tm, CONV_WIDTH),
            pl.BlockSpec((None, tm, PLE_DIM), lambda i: (layer, i, 0)),
            _layer_spec(layer, (D_MODEL, D_MODEL)),
            _layer_spec(layer, (1, D_MODEL)),
            *_ffn_weight_specs(layer),
            _layer_spec(layer, (1, D_MODEL)),
            _layer_spec(layer, (D_MODEL, D_MODEL)),
            _layer_spec(layer, (PLE_DIM, D_MODEL)),
            _layer_spec(layer, (1, D_MODEL)),
        ],
        out_specs=_row_spec(tm, D_MODEL),
        out_shape=jax.ShapeDtypeStruct((n, D_MODEL), F32),
        scratch_shapes=[pltpu.VMEM((tm, D_MODEL), BF16), pltpu.VMEM((tm, D_MODEL), F32)],
        compiler_params=pltpu.CompilerParams(
            dimension_semantics=("parallel",), vmem_limit_bytes=VMEM_LIMIT),
        name="postmix",
    )(x, ysb, ymla, ycv, p, wo, gmix, *ffn_params, gple_pre, wgate, wproj, gple_post)


def _ffn_params(g_pre, w_gate, w_up, w_down, g_post):
    return (g_pre[:, None, :], w_gate.astype(BF16), w_up.astype(BF16), w_down.astype(BF16),
            g_post[:, None, :])


def _pack_in(w_in):
    kr_at = _CV
    kr = w_in[..., kr_at:kr_at + MLA_ROPE]
    kr_tile = jnp.pad(kr, ((0, 0), (0, 0), (MLA_NOPE, LANES - MLA_NOPE - MLA_ROPE)))
    return jnp.concatenate([w_in[..., :kr_at], w_in[..., kr_at + MLA_ROPE:], kr_tile],
                           axis=-1).astype(BF16)


def _pack_uq(w_uq):
    depth = w_uq.shape[0]
    per_head = MLA_NOPE + MLA_ROPE
    w = w_uq.reshape(depth, MLA_Q_RANK, MLA_HEADS, per_head)
    w = jnp.pad(w, ((0, 0), (0, 0), (0, 0), (0, LANES - per_head)))
    return w.reshape(depth, MLA_Q_RANK, MLA_HEADS * LANES).astype(BF16)


def _pack_ukv(w_ukv):
    depth = w_ukv.shape[0]
    w = w_ukv.reshape(depth, MLA_KV_RANK, MLA_HEADS, MLA_NOPE + HEAD_DIM)
    wk = jnp.pad(w[..., :MLA_NOPE], ((0, 0), (0, 0), (0, 0), (0, LANES - MLA_NOPE)))
    wv = w[..., MLA_NOPE:]
    return jnp.concatenate([wk.reshape(depth, MLA_KV_RANK, MLA_HEADS * LANES),
                            wv.reshape(depth, MLA_KV_RANK, MLA_HEADS * HEAD_DIM)],
                           axis=-1).astype(BF16)


def kernel(x, p, positions, g_ffn1_pre, w1_gate, w1_up, w1_down, g_ffn1_post, g_mix_pre, w_in, g_mla_q, w_mla_uq, g_mla_kv, w_mla_ukv, w_conv, w_out, g_mix_post, g_ffn2_pre, w2_gate, w2_up, w2_down, g_ffn2_post, g_ple_pre, w_ple_gate, w_ple_proj, g_ple_post):
    batch, seq, _ = x.shape
    depth = w_in.shape[0]
    n = batch * seq
    assert seq % TOKEN_TILE == 0 and seq % ATT_BLOCK == 0

    half = MLA_ROPE // 2
    inv = ROPE_BASE ** (-jnp.arange(half, dtype=F32) / half)
    zeros = jnp.zeros
    inv_lane = jnp.concatenate([zeros(MLA_NOPE, F32), inv, inv, zeros(LANES - MLA_NOPE - MLA_ROPE, F32)])
    sgn_lane = jnp.concatenate([zeros(MLA_NOPE, F32), -jnp.ones(half, F32), jnp.ones(half, F32),
                                zeros(LANES - MLA_NOPE - MLA_ROPE, F32)])
    cos_t, sin_t = _rope_tables(positions.reshape(n, 1), inv_lane[None, :], sgn_lane[None, :])

    rows = lambda g: g[:, None, :]
    ffn1 = _ffn_params(g_ffn1_pre, w1_gate, w1_up, w1_down, g_ffn1_post)
    ffn2 = _ffn_params(g_ffn2_pre, w2_gate, w2_up, w2_down, g_ffn2_post)
    win, wq, wkv = _pack_in(w_in), _pack_uq(w_mla_uq), _pack_ukv(w_mla_ukv)
    wo, wgate, wproj = w_out.astype(BF16), w_ple_gate.astype(BF16), w_ple_proj.astype(BF16)
    p_rows = p.reshape(depth, n, PLE_DIM)

    xf = x.reshape(n, D_MODEL)
    for i in range(depth):
        (xf, sbq_lo, sbq_hi, sbk, sbv_lo, sbv_hi, mq, mk, mvt_lo, mvt_hi, ycv) = _premix(
            xf, i, ffn1, rows(g_mix_pre), win, rows(g_mla_q), wq, rows(g_mla_kv), wkv, w_conv,
            cos_t, sin_t, seq)
        seqs = lambda t: t.reshape(batch, seq, t.shape[-1])
        ysb = _sb_attention(seqs(sbq_lo), seqs(sbq_hi), seqs(sbk), seqs(sbv_lo), seqs(sbv_hi))
        ymla = _mla_attention(seqs(mq), seqs(mk), mvt_lo, mvt_hi)
        xf = _postmix(
            xf, i, ysb.reshape(n, -1), ymla.reshape(n, -1), ycv, p_rows, wo, rows(g_mix_post), ffn2,
            rows(g_ple_pre), wgate, wproj, rows(g_ple_post))
    return xf.reshape(batch, seq, D_MODEL)
```

```python
import functools

import jax
import jax.numpy as jnp
from jax import lax
from jax.experimental import pallas as pl
from jax.experimental.pallas import tpu as pltpu

F32 = jnp.float32
BF16 = jnp.bfloat16

D_MODEL = 1024
D_FF = 2816
PLE_DIM = 256
HEAD_DIM = 64
SB_WIDTH = 256
MLA_HEADS = 8
MLA_NOPE = 64
MLA_ROPE = 32
MLA_Q_RANK = 384
MLA_KV_RANK = 256
CONV_WIDTH = 256
CONV_K = 3
ROPE_BASE = 10000.0
EPS = 1e-6
LOG2_E = 1.4426950408889634

LANES = 128
FF_CHUNK = 256
N_FF_CHUNKS = D_FF // FF_CHUNK
TOKEN_TILE = 512
ATT_BLOCK = 256
MLA_KV_BLOCK = 512
SB_Q_BLOCK = 1024
MLA_Q_BLOCK = 1024
VMEM_LIMIT = 52 * 1024 * 1024
_SB_Q, _SB_K, _SB_V = 0, 256, 512
_C_Q = 768
_C_KV = _C_Q + MLA_Q_RANK
_CV = _C_KV + MLA_KV_RANK
_KR = _CV + 3 * CONV_WIDTH
IN_PACKED = _KR + LANES


def _rms(x, g):
    return x * lax.rsqrt(jnp.mean(x * x, axis=-1, keepdims=True) + EPS) * g


def _dot(a, b):
    return jnp.dot(a, b, preferred_element_type=F32)


def _dot_nt(a, b):
    return lax.dot_general(a, b, (((1,), (1,)), ((), ())), preferred_element_type=F32)


def _const_spec(shape):
    zeros = (0,) * len(shape)
    return pl.BlockSpec(shape, lambda *_: zeros, pipeline_mode=pl.Buffered(1))


def _layer_spec(layer, shape):
    index = (layer,) + (0,) * len(shape)
    return pl.BlockSpec((None, *shape), lambda *_: index, pipeline_mode=pl.Buffered(1))


def _row_spec(tile, width):
    return pl.BlockSpec((tile, width), lambda i: (i, 0))


def _rope_table_kernel(pos_ref, inv_ref, sgn_ref, cos_ref, sin_ref):
    ang = pos_ref[...].astype(F32) * inv_ref[...]
    cos_ref[...] = jnp.cos(ang)
    sin_ref[...] = jnp.sin(ang) * sgn_ref[...]


def _rope_tables(pos_col, inv_lane, sgn_lane):
    n = pos_col.shape[0]
    tile = 2048
    return pl.pallas_call(
        _rope_table_kernel,
        grid=(n // tile,),
        in_specs=[_row_spec(tile, 1), _const_spec((1, LANES)), _const_spec((1, LANES))],
        out_specs=[_row_spec(tile, LANES), _row_spec(tile, LANES)],
        out_shape=[jax.ShapeDtypeStruct((n, LANES), F32)] * 2,
        compiler_params=pltpu.CompilerParams(dimension_semantics=("parallel",)),
        name="rope_tables",
    )(pos_col, inv_lane, sgn_lane)


def _ffn_body(x, gpre, wg_ref, wu_ref, wd_ref, gpost, h_ref, acc_ref):
    h_ref[...] = _rms(x, gpre).astype(BF16)
    for c in range(N_FF_CHUNKS):
        cols = slice(c * FF_CHUNK, (c + 1) * FF_CHUNK)
        h = h_ref[...]
        g = _dot(h, wg_ref[:, cols])
        u = _dot(h, wu_ref[:, cols])
        a = (g * jax.nn.sigmoid(g) * u).astype(BF16)
        down = _dot(a, wd_ref[cols, :])
        if c == 0:
            acc_ref[...] = down
        else:
            acc_ref[...] += down
    return x + 0.5 * _rms(acc_ref[...], gpost)


def _ffn_weight_specs(layer):
    return [
        _layer_spec(layer, (1, D_MODEL)),
        _layer_spec(layer, (D_MODEL, D_FF)),
        _layer_spec(layer, (D_MODEL, D_FF)),
        _layer_spec(layer, (D_FF, D_MODEL)),
        _layer_spec(layer, (1, D_MODEL)),
    ]


def _swap_rope_halves(blk, lane):
    from_right = pltpu.roll(blk, LANES - MLA_ROPE // 2, axis=1)
    from_left = pltpu.roll(blk, MLA_ROPE // 2, axis=1)
    return jnp.where(lane < MLA_NOPE + MLA_ROPE // 2, from_right, from_left)


def _store_values_transposed(v, lo_ref, hi_ref):
    vt = v.T
    row = lax.broadcasted_iota(jnp.int32, (vt.shape[0], 1), 0)
    even = (row % LANES) < HEAD_DIM
    lo_ref[0] = jnp.where(even, vt, 0.0).astype(BF16)
    hi_ref[0] = jnp.where(even, 0.0, vt).astype(BF16)


def _premix_kernel(tiles_per_seq,
                   x_ref, gpre_ref, wg_ref, wu_ref, wd_ref, gpost_ref,
                   g_ref, win_ref, gq_ref, wq_ref, gkv_ref, wkv_ref, wconv_ref,
                   cos_ref, sin_ref,
                   xmid_ref, sbq_lo_ref, sbq_hi_ref, sbk_ref, sbv_lo_ref, sbv_hi_ref,
                   mq_ref, mk_ref, mv_lo_ref, mv_hi_ref, ycv_ref,
                   h_ref, acc_ref, carry_ref):
    tm = x_ref.shape[0]
    x = _ffn_body(x_ref[...], gpre_ref[...], wg_ref, wu_ref, wd_ref, gpost_ref[...], h_ref, acc_ref)
    xmid_ref[...] = x
    h = _rms(x, g_ref[...]).astype(BF16)
    cos = cos_ref[...]
    sin = sin_ref[...]
    lane = lax.broadcasted_iota(jnp.int32, (1, LANES), 1)

    def even_head_mask(width):
        col = lax.broadcasted_iota(jnp.int32, (1, width), 1)
        return (col % LANES) < HEAD_DIM

    sb = _dot(h, win_ref[:, _SB_Q:_C_Q])
    m_sb = even_head_mask(SB_WIDTH)
    q = sb[:, _SB_Q:_SB_K] * (HEAD_DIM ** -0.5 * LOG2_E)
    sbq_lo_ref[...] = jnp.where(m_sb, q, 0.0).astype(BF16)
    sbq_hi_ref[...] = jnp.where(m_sb, 0.0, q).astype(BF16)
    sbk_ref[...] = sb[:, _SB_K:_SB_V].astype(BF16)
    v = sb[:, _SB_V:_C_Q]
    sbv_lo_ref[...] = jnp.where(m_sb, v, 0.0).astype(BF16)
    sbv_hi_ref[...] = jnp.where(m_sb, 0.0, v).astype(BF16)

    cq = _dot(h, win_ref[:, _C_Q:_C_KV])
    q_raw = _dot(_rms(cq, gq_ref[...]).astype(BF16), wq_ref[...])
    q_scale = (MLA_NOPE + MLA_ROPE) ** -0.5 * LOG2_E
    for hd in range(MLA_HEADS):
        blk = q_raw[:, hd * LANES:(hd + 1) * LANES]
        roped = blk * cos + _swap_rope_halves(blk, lane) * sin
        mq_ref[:, hd * LANES:(hd + 1) * LANES] = (roped * q_scale).astype(BF16)

    ckv = _dot(h, win_ref[:, _C_KV:_CV])
    kv = _dot(_rms(ckv, gkv_ref[...]).astype(BF16), wkv_ref[...])
    kr = _dot(h, win_ref[:, _KR:IN_PACKED])
    kr = kr * cos + _swap_rope_halves(kr, lane) * sin
    for hd in range(MLA_HEADS):
        mk_ref[:, hd * LANES:(hd + 1) * LANES] = (
            kv[:, hd * LANES:(hd + 1) * LANES] + kr).astype(BF16)
    _store_values_transposed(kv[:, MLA_HEADS * LANES:], mv_lo_ref, mv_hi_ref)

    cv = _dot(h, win_ref[:, _CV:_KR])
    u = cv[:, CONV_WIDTH:2 * CONV_WIDTH] * cv[:, 2 * CONV_WIDTH:]
    seq_start = (pl.program_id(0) % tiles_per_seq) == 0
    prev = jnp.where(seq_start, 0.0, carry_ref[...])
    row = lax.broadcasted_iota(jnp.int32, (tm, CONV_WIDTH), 0)
    u1 = jnp.where(row == 0, prev[7:8, :], pltpu.roll(u, 1, axis=0))
    u2 = jnp.where(row == 0, prev[6:7, :],
                   jnp.where(row == 1, prev[7:8, :], pltpu.roll(u, 2, axis=0)))
    w = wconv_ref[...]
    y = cv[:, :CONV_WIDTH] * (w[2:3, :] * u + w[1:2, :] * u1 + w[0:1, :] * u2)
    ycv_ref[...] = y.astype(BF16)
    carry_ref[...] = u[tm - 8:, :]


def _premix(x, layer, ffn_params, g, win, gq, wq, gkv, wkv, wconv, cos_t, sin_t, seq_len):
    n = x.shape[0]
    tm = TOKEN_TILE
    tiles_per_seq = seq_len // tm
    mla_v = MLA_HEADS * HEAD_DIM

    def rows(width):
        return _row_spec(tm, width), jax.ShapeDtypeStruct((n, width), BF16)

    def feature_major(width):
        spec = pl.BlockSpec((1, width, tm), lambda i: (i // tiles_per_seq, 0, i % tiles_per_seq))
        return spec, jax.ShapeDtypeStruct((n // seq_len, width, seq_len), BF16)

    outs = [(_row_spec(tm, D_MODEL), jax.ShapeDtypeStruct((n, D_MODEL), F32)),
            rows(SB_WIDTH), rows(SB_WIDTH), rows(SB_WIDTH), rows(SB_WIDTH), rows(SB_WIDTH),
            rows(MLA_HEADS * LANES), rows(MLA_HEADS * LANES),
            feature_major(mla_v), feature_major(mla_v), rows(CONV_WIDTH)]
    return pl.pallas_call(
        functools.partial(_premix_kernel, tiles_per_seq),
        grid=(n // tm,),
        in_specs=[
            _row_spec(tm, D_MODEL),
            *_ffn_weight_specs(layer),
            _layer_spec(layer, (1, D_MODEL)),
            _layer_spec(layer, (D_MODEL, IN_PACKED)),
            _layer_spec(layer, (1, MLA_Q_RANK)),
            _layer_spec(layer, (MLA_Q_RANK, MLA_HEADS * LANES)),
            _layer_spec(layer, (1, MLA_KV_RANK)),
            _layer_spec(layer, (MLA_KV_RANK, MLA_HEADS * (LANES + HEAD_DIM))),
            _layer_spec(layer, (CONV_K, CONV_WIDTH)),
            _row_spec(tm, LANES),
            _row_spec(tm, LANES),
        ],
        out_specs=[spec for spec, _ in outs],
        out_shape=[shape for _, shape in outs],
        scratch_shapes=[pltpu.VMEM((tm, D_MODEL), BF16), pltpu.VMEM((tm, D_MODEL), F32),
                        pltpu.VMEM((8, CONV_WIDTH), F32)],
        compiler_params=pltpu.CompilerParams(
            dimension_semantics=("arbitrary",), vmem_limit_bytes=VMEM_LIMIT),
        name="premix",
    )(x, *ffn_params, g, win, gq, wq, gkv, wkv, wconv, cos_t, sin_t)


def _sb_kernel(q_lo_ref, q_hi_ref, k_ref, v_lo_ref, v_hi_ref, o_ref):
    blk = ATT_BLOCK
    tq = q_lo_ref.shape[1]
    diag_blocks = tq // blk
    qi = pl.program_id(1)
    r = lax.broadcasted_iota(jnp.int32, (blk, blk), 0)
    c = lax.broadcasted_iota(jnp.int32, (blk, blk), 1)
    neg_tri = jnp.where(r > c, -1.0, 0.0).astype(BF16)

    def rows_from(q0, old, new):
        return new if q0 == 0 else jnp.concatenate([old[:q0], new], axis=0)

    def weights(q_ref, rows, cols, carry, q0, masked):
        z2 = _dot_nt(q_ref[0, q0:, cols], k_ref[0, rows, cols])
        sp2 = jnp.maximum(z2, 0.0) + jnp.log2(1.0 + jnp.exp2(-jnp.abs(z2)))
        if masked:
            qry = lax.broadcasted_iota(jnp.int32, z2.shape, 0)
            key = lax.broadcasted_iota(jnp.int32, z2.shape, 1)
            visible = key < qry
            sp2_in = jnp.where(visible, sp2, 0.0)
        else:
            sp2_in = sp2
        hi = sp2_in.astype(BF16)
        lo = (sp2_in - hi.astype(F32)).astype(BF16)
        later = _dot(hi, neg_tri) + _dot(lo, neg_tri)
        w = jnp.exp2((z2 - sp2) + (later + carry[q0:]))
        if masked:
            w = jnp.where(visible, w, 0.0)
        new_carry = carry[q0:] + (later[:, 0:1] - sp2_in[:, 0:1])
        return w.astype(BF16), rows_from(q0, carry, new_carry)

    n_pairs = SB_WIDTH // LANES

    def all_heads(rows, state, q0=0, masked=False):
        carries, accs = state
        new_carries, new_accs = [], []
        for pair in range(n_pairs):
            cols = slice(pair * LANES, (pair + 1) * LANES)
            w_lo, c_lo = weights(q_lo_ref, rows, cols, carries[2 * pair], q0, masked)
            w_hi, c_hi = weights(q_hi_ref, rows, cols, carries[2 * pair + 1], q0, masked)
            new_carries += [c_lo, c_hi]
            acc = accs[pair][q0:] + (_dot(w_lo, v_lo_ref[0, rows, cols])
                                     + _dot(w_hi, v_hi_ref[0, rows, cols]))
            new_accs.append(rows_from(q0, accs[pair], acc))
        return tuple(new_carries), tuple(new_accs)

    state = (tuple(jnp.zeros((tq, 1), F32) for _ in range(2 * n_pairs)),
             tuple(jnp.zeros((tq, LANES), F32) for _ in range(n_pairs)))
    for d in reversed(range(diag_blocks)):
        rows = pl.ds(pl.multiple_of((qi * diag_blocks + d) * blk, blk), blk)
        state = all_heads(rows, state, q0=d * blk, masked=True)

    def step(t, state):
        return all_heads(pl.ds(pl.multiple_of((qi * diag_blocks - 1 - t) * blk, blk), blk), state)

    _, accs = lax.fori_loop(0, qi * diag_blocks, step, state)
    for pair in range(n_pairs):
        o_ref[0, :, pair * LANES:(pair + 1) * LANES] = accs[pair].astype(BF16)


def _sb_attention(q_lo, q_hi, k, v_lo, v_hi):
    b, s, w = k.shape
    q_spec = pl.BlockSpec((1, SB_Q_BLOCK, w), lambda bi, qi: (bi, qi, 0))
    kv_spec = pl.BlockSpec((1, s, w), lambda bi, qi: (bi, 0, 0))
    return pl.pallas_call(
        _sb_kernel,
        grid=(b, s // SB_Q_BLOCK),
        in_specs=[q_spec, q_spec, kv_spec, kv_spec, kv_spec],
        out_specs=q_spec,
        out_shape=jax.ShapeDtypeStruct((b, s, w), BF16),
        compiler_params=pltpu.CompilerParams(
            dimension_semantics=("parallel", "arbitrary"), vmem_limit_bytes=VMEM_LIMIT),
        name="sb_attention",
    )(q_lo, q_hi, k, v_lo, v_hi)


def _mla_kernel(q_ref, k_ref, vt_lo_ref, vt_hi_ref, o_ref):
    blk = MLA_KV_BLOCK
    tq = q_ref.shape[1]
    diag_blocks = tq // blk
    qi = pl.program_id(1)
    n_pairs = MLA_HEADS // 2
    even_rows = lax.broadcasted_iota(jnp.int32, (LANES, 1), 0) < HEAD_DIM

    def lanes_from(q0, old, new):
        return new if q0 == 0 else jnp.concatenate([old[:, :q0], new], axis=1)

    def probs(hd, rows, m, l, q0, masked):
        hcols = slice(hd * LANES, (hd + 1) * LANES)
        s = _dot_nt(k_ref[0, rows, hcols], q_ref[0, q0:, hcols])
        if masked:
            key = lax.broadcasted_iota(jnp.int32, s.shape, 0)
            qry = lax.broadcasted_iota(jnp.int32, s.shape, 1)
            s = jnp.where(key <= qry, s, -1e30)
        if m is None:
            m_new = jnp.max(s, axis=0, keepdims=True)
            p = jnp.exp2(s - m_new)
            return p.astype(BF16), m_new, jnp.sum(p, axis=0, keepdims=True), None
        m_new = jnp.maximum(m[:, q0:], jnp.max(s, axis=0, keepdims=True))
        alpha = jnp.exp2(m[:, q0:] - m_new)
        p = jnp.exp2(s - m_new)
        l_new = alpha * l[:, q0:] + jnp.sum(p, axis=0, keepdims=True)
        return p.astype(BF16), lanes_from(q0, m, m_new), lanes_from(q0, l, l_new), alpha

    def all_heads(rows, state, q0=0, masked=False):
        ms, ls, accs = state
        new_ms, new_ls, new_accs = [], [], []
        for pair in range(n_pairs):
            vrows = slice(pair * LANES, (pair + 1) * LANES)
            lo, hi = 2 * pair, 2 * pair + 1
            p_lo, m_lo, l_lo, a_lo = probs(lo, rows, ms[lo], ls[lo], q0, masked)
            p_hi, m_hi, l_hi, a_hi = probs(hi, rows, ms[hi], ls[hi], q0, masked)
            pv = _dot(vt_lo_ref[0, vrows, rows], p_lo) + _dot(vt_hi_ref[0, vrows, rows], p_hi)
            if accs[pair] is not None:
                pv = lanes_from(q0, accs[pair],
                                accs[pair][:, q0:] * jnp.where(even_rows, a_lo, a_hi) + pv)
            new_ms += [m_lo, m_hi]
            new_ls += [l_lo, l_hi]
            new_accs.append(pv)
        return tuple(new_ms), tuple(new_ls), tuple(new_accs)

    none = (None,) * MLA_HEADS
    state = (none, none, none[:n_pairs])
    for d in range(diag_blocks):
        rows = pl.ds(pl.multiple_of((qi * diag_blocks + d) * blk, blk), blk)
        state = all_heads(rows, state, q0=d * blk, masked=True)

    def step(t, state):
        return all_heads(pl.ds(pl.multiple_of(t * blk, blk), blk), state)

    _, ls, accs = lax.fori_loop(0, qi * diag_blocks, step, state)
    for pair in range(n_pairs):
        denom = jnp.where(even_rows, ls[2 * pair], ls[2 * pair + 1])
        o_ref[0, :, pair * LANES:(pair + 1) * LANES] = (accs[pair] / denom).T.astype(BF16)


def _mla_attention(q, k, vt_lo, vt_hi):
    b, s, wq = q.shape
    wv = vt_lo.shape[1]
    return pl.pallas_call(
        _mla_kernel,
        grid=(b, s // MLA_Q_BLOCK),
        in_specs=[
            pl.BlockSpec((1, MLA_Q_BLOCK, wq), lambda bi, qi: (bi, qi, 0)),
            pl.BlockSpec((1, s, wq), lambda bi, qi: (bi, 0, 0)),
            pl.BlockSpec((1, wv, s), lambda bi, qi: (bi, 0, 0)),
            pl.BlockSpec((1, wv, s), lambda bi, qi: (bi, 0, 0)),
        ],
        out_specs=pl.BlockSpec((1, MLA_Q_BLOCK, wv), lambda bi, qi: (bi, qi, 0)),
        out_shape=jax.ShapeDtypeStruct((b, s, wv), BF16),
        compiler_params=pltpu.CompilerParams(
            dimension_semantics=("parallel", "arbitrary"), vmem_limit_bytes=VMEM_LIMIT),
        name="mla_attention",
    )(q, k, vt_lo, vt_hi)


def _postmix_kernel(x_ref, ysb_ref, ymla_ref, ycv_ref, p_ref, wo_ref, gmix_ref,
                    gpre_ref, wg_ref, wu_ref, wd_ref, gpost_ref,
                    gple_pre_ref, wgate_ref, wproj_ref, gple_post_ref,
                    o_ref, h_ref, acc_ref):
    sb_rows = SB_WIDTH
    mla_rows = SB_WIDTH + MLA_HEADS * HEAD_DIM
    mixed = (_dot(ysb_ref[...], wo_ref[:sb_rows, :])
             + _dot(ymla_ref[...], wo_ref[sb_rows:mla_rows, :])
             + _dot(ycv_ref[...], wo_ref[mla_rows:, :]))
    x = x_ref[...] + _rms(mixed, gmix_ref[...])
    x = _ffn_body(x, gpre_ref[...], wg_ref, wu_ref, wd_ref, gpost_ref[...], h_ref, acc_ref)
    h = _rms(x, gple_pre_ref[...]).astype(BF16)
    e = jax.nn.sigmoid(_dot(h, wgate_ref[...])) * _dot(p_ref[...].astype(BF16), wproj_ref[...])
    o_ref[...] = x + _rms(e, gple_post_ref[...])


def _postmix(x, layer, ysb, ymla, ycv, p, wo, gmix, ffn_params, gple_pre, wgate, wproj, gple_post):
    n = x.shape[0]
    tm = TOKEN_TILE
    return pl.pallas_call(
        _postmix_kernel,
        grid=(n // tm,),
        in_specs=[
            _row_spec(tm, D_MODEL),
            _row_spec(tm, SB_WIDTH),
            _row_spec(tm, MLA_HEADS * HEAD_DIM),
            _row_spec(tm, CONV_WIDTH),
            pl.BlockSpec((None, tm, PLE_DIM), lambda i: (layer, i, 0)),
            _layer_spec(layer, (D_MODEL, D_MODEL)),
            _layer_spec(layer, (1, D_MODEL)),
            *_ffn_weight_specs(layer),
            _layer_spec(layer, (1, D_MODEL)),
            _layer_spec(layer, (D_MODEL, D_MODEL)),
            _layer_spec(layer, (PLE_DIM, D_MODEL)),
            _layer_spec(layer, (1, D_MODEL)),
        ],
        out_specs=_row_spec(tm, D_MODEL),
        out_shape=jax.ShapeDtypeStruct((n, D_MODEL), F32),
        scratch_shapes=[pltpu.VMEM((tm, D_MODEL), BF16), pltpu.VMEM((tm, D_MODEL), F32)],
        compiler_params=pltpu.CompilerParams(
            dimension_semantics=("parallel",), vmem_limit_bytes=VMEM_LIMIT),
        name="postmix",
    )(x, ysb, ymla, ycv, p, wo, gmix, *ffn_params, gple_pre, wgate, wproj, gple_post)


def _ffn_params(g_pre, w_gate, w_up, w_down, g_post):
    return (g_pre[:, None, :], w_gate.astype(BF16), w_up.astype(BF16), w_down.astype(BF16),
            g_post[:, None, :])


def _pack_in(w_in):
    kr_at = _CV
    kr = w_in[..., kr_at:kr_at + MLA_ROPE]
    kr_tile = jnp.pad(kr, ((0, 0), (0, 0), (MLA_NOPE, LANES - MLA_NOPE - MLA_ROPE)))
    return jnp.concatenate([w_in[..., :kr_at], w_in[..., kr_at + MLA_ROPE:], kr_tile],
                           axis=-1).astype(BF16)


def _pack_uq(w_uq):
    depth = w_uq.shape[0]
    per_head = MLA_NOPE + MLA_ROPE
    w = w_uq.reshape(depth, MLA_Q_RANK, MLA_HEADS, per_head)
    w = jnp.pad(w, ((0, 0), (0, 0), (0, 0), (0, LANES - per_head)))
    return w.reshape(depth, MLA_Q_RANK, MLA_HEADS * LANES).astype(BF16)


def _pack_ukv(w_ukv):
    depth = w_ukv.shape[0]
    w = w_ukv.reshape(depth, MLA_KV_RANK, MLA_HEADS, MLA_NOPE + HEAD_DIM)
    wk = jnp.pad(w[..., :MLA_NOPE], ((0, 0), (0, 0), (0, 0), (0, LANES - MLA_NOPE)))
    wv = w[..., MLA_NOPE:]
    return jnp.concatenate([wk.reshape(depth, MLA_KV_RANK, MLA_HEADS * LANES),
                            wv.reshape(depth, MLA_KV_RANK, MLA_HEADS * HEAD_DIM)],
                           axis=-1).astype(BF16)


def kernel(x, p, positions, g_ffn1_pre, w1_gate, w1_up, w1_down, g_ffn1_post, g_mix_pre, w_in, g_mla_q, w_mla_uq, g_mla_kv, w_mla_ukv, w_conv, w_out, g_mix_post, g_ffn2_pre, w2_gate, w2_up, w2_down, g_ffn2_post, g_ple_pre, w_ple_gate, w_ple_proj, g_ple_post):
    batch, seq, _ = x.shape
    depth = w_in.shape[0]
    n = batch * seq
    assert seq % TOKEN_TILE == 0 and seq % ATT_BLOCK == 0

    half = MLA_ROPE // 2
    inv = ROPE_BASE ** (-jnp.arange(half, dtype=F32) / half)
    zeros = jnp.zeros
    inv_lane = jnp.concatenate([zeros(MLA_NOPE, F32), inv, inv, zeros(LANES - MLA_NOPE - MLA_ROPE, F32)])
    sgn_lane = jnp.concatenate([zeros(MLA_NOPE, F32), -jnp.ones(half, F32), jnp.ones(half, F32),
                                zeros(LANES - MLA_NOPE - MLA_ROPE, F32)])
    cos_t, sin_t = _rope_tables(positions.reshape(n, 1), inv_lane[None, :], sgn_lane[None, :])

    rows = lambda g: g[:, None, :]
    ffn1 = _ffn_params(g_ffn1_pre, w1_gate, w1_up, w1_down, g_ffn1_post)
    ffn2 = _ffn_params(g_ffn2_pre, w2_gate, w2_up, w2_down, g_ffn2_post)
    win, wq, wkv = _pack_in(w_in), _pack_uq(w_mla_uq), _pack_ukv(w_mla_ukv)
    wo, wgate, wproj = w_out.astype(BF16), w_ple_gate.astype(BF16), w_ple_proj.astype(BF16)
    p_rows = p.reshape(depth, n, PLE_DIM)

    xf = x.reshape(n, D_MODEL)
    for i in range(depth):
        (xf, sbq_lo, sbq_hi, sbk, sbv_lo, sbv_hi, mq, mk, mvt_lo, mvt_hi, ycv) = _premix(
            xf, i, ffn1, rows(g_mix_pre), win, rows(g_mla_q), wq, rows(g_mla_kv), wkv, w_conv,
            cos_t, sin_t, seq)
        seqs = lambda t: t.reshape(batch, seq, t.shape[-1])
        ysb = _sb_attention(seqs(sbq_lo), seqs(sbq_hi), seqs(sbk), seqs(sbv_lo), seqs(sbv_hi))
        ymla = _mla_attention(seqs(mq), seqs(mk), mvt_lo, mvt_hi)
        xf = _postmix(
            xf, i, ysb.reshape(n, -1), ymla.reshape(n, -1), ycv, p_rows, wo, rows(g_mix_post), ffn2,
            rows(g_ple_pre), wgate, wproj, rows(g_ple_post))
    return xf.reshape(batch, seq, D_MODEL)
```

```python
import functools

import jax
import jax.numpy as jnp
from jax import lax
from jax.experimental import pallas as pl
from jax.experimental.pallas import tpu as pltpu

F32 = jnp.float32
BF16 = jnp.bfloat16

D_MODEL = 1024
D_FF = 2816
PLE_DIM = 256
HEAD_DIM = 64
SB_WIDTH = 256
MLA_HEADS = 8
MLA_NOPE = 64
MLA_ROPE = 32
MLA_Q_RANK = 384
MLA_KV_RANK = 256
CONV_WIDTH = 256
CONV_K = 3
ROPE_BASE = 10000.0
EPS = 1e-6
LOG2_E = 1.4426950408889634

LANES = 128
FF_CHUNK = 256
N_FF_CHUNKS = D_FF // FF_CHUNK
TOKEN_TILE = 512
ATT_BLOCK = 256
MLA_KV_BLOCK = 512
SB_Q_BLOCK = 2048
MLA_Q_BLOCK = 1024
VMEM_LIMIT = 52 * 1024 * 1024
_SB_Q, _SB_K, _SB_V = 0, 256, 512
_C_Q = 768
_C_KV = _C_Q + MLA_Q_RANK
_CV = _C_KV + MLA_KV_RANK
_KR = _CV + 3 * CONV_WIDTH
IN_PACKED = _KR + LANES


def _rms(x, g):
    return x * lax.rsqrt(jnp.mean(x * x, axis=-1, keepdims=True) + EPS) * g


def _dot(a, b):
    return jnp.dot(a, b, preferred_element_type=F32)


def _dot_nt(a, b):
    return lax.dot_general(a, b, (((1,), (1,)), ((), ())), preferred_element_type=F32)


def _const_spec(shape):
    zeros = (0,) * len(shape)
    return pl.BlockSpec(shape, lambda *_: zeros, pipeline_mode=pl.Buffered(1))


def _layer_spec(layer, shape):
    index = (layer,) + (0,) * len(shape)
    return pl.BlockSpec((None, *shape), lambda *_: index, pipeline_mode=pl.Buffered(1))


def _row_spec(tile, width):
    return pl.BlockSpec((tile, width), lambda i: (i, 0))


def _rope_table_kernel(pos_ref, inv_ref, sgn_ref, cos_ref, sin_ref):
    ang = pos_ref[...].astype(F32) * inv_ref[...]
    cos_ref[...] = jnp.cos(ang)
    sin_ref[...] = jnp.sin(ang) * sgn_ref[...]


def _rope_tables(pos_col, inv_lane, sgn_lane):
    n = pos_col.shape[0]
    tile = 2048
    return pl.pallas_call(
        _rope_table_kernel,
        grid=(n // tile,),
        in_specs=[_row_spec(tile, 1), _const_spec((1, LANES)), _const_spec((1, LANES))],
        out_specs=[_row_spec(tile, LANES), _row_spec(tile, LANES)],
        out_shape=[jax.ShapeDtypeStruct((n, LANES), F32)] * 2,
        compiler_params=pltpu.CompilerParams(dimension_semantics=("parallel",)),
        name="rope_tables",
    )(pos_col, inv_lane, sgn_lane)


def _ffn_body(x, gpre, wg_ref, wu_ref, wd_ref, gpost, h_ref, acc_ref):
    h_ref[...] = _rms(x, gpre).astype(BF16)
    for c in range(N_FF_CHUNKS):
        cols = slice(c * FF_CHUNK, (c + 1) * FF_CHUNK)
        h = h_ref[...]
        g = _dot(h, wg_ref[:, cols])
        u = _dot(h, wu_ref[:, cols])
        a = (g * jax.nn.sigmoid(g) * u).astype(BF16)
        down = _dot(a, wd_ref[cols, :])
        if c == 0:
            acc_ref[...] = down
        else:
            acc_ref[...] += down
    return x + 0.5 * _rms(acc_ref[...], gpost)


def _ffn_weight_specs(layer):
    return [
        _layer_spec(layer, (1, D_MODEL)),
        _layer_spec(layer, (D_MODEL, D_FF)),
        _layer_spec(layer, (D_MODEL, D_FF)),
        _layer_spec(layer, (D_FF, D_MODEL)),
        _layer_spec(layer, (1, D_MODEL)),
    ]


def _swap_rope_halves(blk, lane):
    from_right = pltpu.roll(blk, LANES - MLA_ROPE // 2, axis=1)
    from_left = pltpu.roll(blk, MLA_ROPE // 2, axis=1)
    return jnp.where(lane < MLA_NOPE + MLA_ROPE // 2, from_right, from_left)


def _store_values_transposed(v, lo_ref, hi_ref):
    vt = v.T
    row = lax.broadcasted_iota(jnp.int32, (vt.shape[0], 1), 0)
    even = (row % LANES) < HEAD_DIM
    lo_ref[0] = jnp.where(even, vt, 0.0).astype(BF16)
    hi_ref[0] = jnp.where(even, 0.0, vt).astype(BF16)


def _premix_kernel(tiles_per_seq,
                   x_ref, gpre_ref, wg_ref, wu_ref, wd_ref, gpost_ref,
                   g_ref, win_ref, gq_ref, wq_ref, gkv_ref, wkv_ref, wconv_ref,
                   cos_ref, sin_ref,
                   xmid_ref, sbq_lo_ref, sbq_hi_ref, sbk_ref, sbv_lo_ref, sbv_hi_ref,
                   mq_ref, mk_ref, mv_lo_ref, mv_hi_ref, ycv_ref,
                   h_ref, acc_ref, carry_ref):
    tm = x_ref.shape[0]
    x = _ffn_body(x_ref[...], gpre_ref[...], wg_ref, wu_ref, wd_ref, gpost_ref[...], h_ref, acc_ref)
    xmid_ref[...] = x
    h = _rms(x, g_ref[...]).astype(BF16)
    cos = cos_ref[...]
    sin = sin_ref[...]
    lane = lax.broadcasted_iota(jnp.int32, (1, LANES), 1)

    def even_head_mask(width):
        col = lax.broadcasted_iota(jnp.int32, (1, width), 1)
        return (col % LANES) < HEAD_DIM

    sb = _dot(h, win_ref[:, _SB_Q:_C_Q])
    m_sb = even_head_mask(SB_WIDTH)
    q = sb[:, _SB_Q:_SB_K] * (HEAD_DIM ** -0.5 * LOG2_E)
    sbq_lo_ref[...] = jnp.where(m_sb, q, 0.0).astype(BF16)
    sbq_hi_ref[...] = jnp.where(m_sb, 0.0, q).astype(BF16)
    sbk_ref[...] = sb[:, _SB_K:_SB_V].astype(BF16)
    v = sb[:, _SB_V:_C_Q]
    sbv_lo_ref[...] = jnp.where(m_sb, v, 0.0).astype(BF16)
    sbv_hi_ref[...] = jnp.where(m_sb, 0.0, v).astype(BF16)

    cq = _dot(h, win_ref[:, _C_Q:_C_KV])
    q_raw = _dot(_rms(cq, gq_ref[...]).astype(BF16), wq_ref[...])
    q_scale = (MLA_NOPE + MLA_ROPE) ** -0.5 * LOG2_E
    for hd in range(MLA_HEADS):
        blk = q_raw[:, hd * LANES:(hd + 1) * LANES]
        roped = blk * cos + _swap_rope_halves(blk, lane) * sin
        mq_ref[:, hd * LANES:(hd + 1) * LANES] = (roped * q_scale).astype(BF16)

    ckv = _dot(h, win_ref[:, _C_KV:_CV])
    kv = _dot(_rms(ckv, gkv_ref[...]).astype(BF16), wkv_ref[...])
    kr = _dot(h, win_ref[:, _KR:IN_PACKED])
    kr = kr * cos + _swap_rope_halves(kr, lane) * sin
    for hd in range(MLA_HEADS):
        mk_ref[:, hd * LANES:(hd + 1) * LANES] = (
            kv[:, hd * LANES:(hd + 1) * LANES] + kr).astype(BF16)
    _store_values_transposed(kv[:, MLA_HEADS * LANES:], mv_lo_ref, mv_hi_ref)

    cv = _dot(h, win_ref[:, _CV:_KR])
    u = cv[:, CONV_WIDTH:2 * CONV_WIDTH] * cv[:, 2 * CONV_WIDTH:]
    seq_start = (pl.program_id(0) % tiles_per_seq) == 0
    prev = jnp.where(seq_start, 0.0, carry_ref[...])
    row = lax.broadcasted_iota(jnp.int32, (tm, CONV_WIDTH), 0)
    u1 = jnp.where(row == 0, prev[7:8, :], pltpu.roll(u, 1, axis=0))
    u2 = jnp.where(row == 0, prev[6:7, :],
                   jnp.where(row == 1, prev[7:8, :], pltpu.roll(u, 2, axis=0)))
    w = wconv_ref[...]
    y = cv[:, :CONV_WIDTH] * (w[2:3, :] * u + w[1:2, :] * u1 + w[0:1, :] * u2)
    ycv_ref[...] = y.astype(BF16)
    carry_ref[...] = u[tm - 8:, :]


def _premix(x, layer, ffn_params, g, win, gq, wq, gkv, wkv, wconv, cos_t, sin_t, seq_len):
    n = x.shape[0]
    tm = TOKEN_TILE
    tiles_per_seq = seq_len // tm
    mla_v = MLA_HEADS * HEAD_DIM

    def rows(width):
        return _row_spec(tm, width), jax.ShapeDtypeStruct((n, width), BF16)

    def feature_major(width):
        spec = pl.BlockSpec((1, width, tm), lambda i: (i // tiles_per_seq, 0, i % tiles_per_seq))
        return spec, jax.ShapeDtypeStruct((n // seq_len, width, seq_len), BF16)

    outs = [(_row_spec(tm, D_MODEL), jax.ShapeDtypeStruct((n, D_MODEL), F32)),
            rows(SB_WIDTH), rows(SB_WIDTH), rows(SB_WIDTH), rows(SB_WIDTH), rows(SB_WIDTH),
            rows(MLA_HEADS * LANES), rows(MLA_HEADS * LANES),
            feature_major(mla_v), feature_major(mla_v), rows(CONV_WIDTH)]
    return pl.pallas_call(
        functools.partial(_premix_kernel, tiles_per_seq),
        grid=(n // tm,),
        in_specs=[
            _row_spec(tm, D_MODEL),
            *_ffn_weight_specs(layer),
            _layer_spec(layer, (1, D_MODEL)),
            _layer_spec(layer, (D_MODEL, IN_PACKED)),
            _layer_spec(layer, (1, MLA_Q_RANK)),
            _layer_spec(layer, (MLA_Q_RANK, MLA_HEADS * LANES)),
            _layer_spec(layer, (1, MLA_KV_RANK)),
            _layer_spec(layer, (MLA_KV_RANK, MLA_HEADS * (LANES + HEAD_DIM))),
            _layer_spec(layer, (CONV_K, CONV_WIDTH)),
            _row_spec(tm, LANES),
            _row_spec(tm, LANES),
        ],
        out_specs=[spec for spec, _ in outs],
        out_shape=[shape for _, shape in outs],
        scratch_shapes=[pltpu.VMEM((tm, D_MODEL), BF16), pltpu.VMEM((tm, D_MODEL), F32),
                        pltpu.VMEM((8, CONV_WIDTH), F32)],
        compiler_params=pltpu.CompilerParams(
            dimension_semantics=("arbitrary",), vmem_limit_bytes=VMEM_LIMIT),
        name="premix",
    )(x, *ffn_params, g, win, gq, wq, gkv, wkv, wconv, cos_t, sin_t)


def _sb_kernel(q_lo_ref, q_hi_ref, k_ref, v_lo_ref, v_hi_ref, o_ref):
    blk = ATT_BLOCK
    tq = q_lo_ref.shape[1]
    diag_blocks = tq // blk
    qi = pl.program_id(1)
    r = lax.broadcasted_iota(jnp.int32, (blk, blk), 0)
    c = lax.broadcasted_iota(jnp.int32, (blk, blk), 1)
    neg_tri = jnp.where(r > c, -1.0, 0.0).astype(BF16)

    def rows_from(q0, old, new):
        return new if q0 == 0 else jnp.concatenate([old[:q0], new], axis=0)

    def weights(q_ref, rows, cols, carry, q0, masked):
        z2 = _dot_nt(q_ref[0, q0:, cols], k_ref[0, rows, cols])
        sp2 = jnp.maximum(z2, 0.0) + jnp.log2(1.0 + jnp.exp2(-jnp.abs(z2)))
        if masked:
            qry = lax.broadcasted_iota(jnp.int32, z2.shape, 0)
            key = lax.broadcasted_iota(jnp.int32, z2.shape, 1)
            visible = key < qry
            sp2_in = jnp.where(visible, sp2, 0.0)
        else:
            sp2_in = sp2
        hi = sp2_in.astype(BF16)
        lo = (sp2_in - hi.astype(F32)).astype(BF16)
        later = _dot(hi, neg_tri) + _dot(lo, neg_tri)
        w = jnp.exp2((z2 - sp2) + (later + carry[q0:]))
        if masked:
            w = jnp.where(visible, w, 0.0)
        new_carry = carry[q0:] + (later[:, 0:1] - sp2_in[:, 0:1])
        return w.astype(BF16), rows_from(q0, carry, new_carry)

    n_pairs = SB_WIDTH // LANES

    def all_heads(rows, state, q0=0, masked=False):
        carries, accs = state
        new_carries, new_accs = [], []
        for pair in range(n_pairs):
            cols = slice(pair * LANES, (pair + 1) * LANES)
            w_lo, c_lo = weights(q_lo_ref, rows, cols, carries[2 * pair], q0, masked)
            w_hi, c_hi = weights(q_hi_ref, rows, cols, carries[2 * pair + 1], q0, masked)
            new_carries += [c_lo, c_hi]
            acc = accs[pair][q0:] + (_dot(w_lo, v_lo_ref[0, rows, cols])
                                     + _dot(w_hi, v_hi_ref[0, rows, cols]))
            new_accs.append(rows_from(q0, accs[pair], acc))
        return tuple(new_carries), tuple(new_accs)

    state = (tuple(jnp.zeros((tq, 1), F32) for _ in range(2 * n_pairs)),
             tuple(jnp.zeros((tq, LANES), F32) for _ in range(n_pairs)))
    for d in reversed(range(diag_blocks)):
        rows = pl.ds(pl.multiple_of((qi * diag_blocks + d) * blk, blk), blk)
        state = all_heads(rows, state, q0=d * blk, masked=True)

    def step(t, state):
        return all_heads(pl.ds(pl.multiple_of((qi * diag_blocks - 1 - t) * blk, blk), blk), state)

    _, accs = lax.fori_loop(0, qi * diag_blocks, step, state)
    for pair in range(n_pairs):
        o_ref[0, :, pair * LANES:(pair + 1) * LANES] = accs[pair].astype(BF16)


def _sb_attention(q_lo, q_hi, k, v_lo, v_hi):
    b, s, w = k.shape
    q_spec = pl.BlockSpec((1, SB_Q_BLOCK, w), lambda bi, qi: (bi, qi, 0))
    kv_spec = pl.BlockSpec((1, s, w), lambda bi, qi: (bi, 0, 0))
    return pl.pallas_call(
        _sb_kernel,
        grid=(b, s // SB_Q_BLOCK),
        in_specs=[q_spec, q_spec, kv_spec, kv_spec, kv_spec],
        out_specs=q_spec,
        out_shape=jax.ShapeDtypeStruct((b, s, w), BF16),
        compiler_params=pltpu.CompilerParams(
            dimension_semantics=("parallel", "arbitrary"), vmem_limit_bytes=VMEM_LIMIT),
        name="sb_attention",
    )(q_lo, q_hi, k, v_lo, v_hi)


def _mla_kernel(q_ref, k_ref, vt_lo_ref, vt_hi_ref, o_ref):
    blk = MLA_KV_BLOCK
    tq = q_ref.shape[1]
    diag_blocks = tq // blk
    qi = pl.program_id(1)
    n_pairs = MLA_HEADS // 2
    even_rows = lax.broadcasted_iota(jnp.int32, (LANES, 1), 0) < HEAD_DIM

    def lanes_from(q0, old, new):
        return new if q0 == 0 else jnp.concatenate([old[:, :q0], new], axis=1)

    def probs(hd, rows, m, l, q0, masked):
        hcols = slice(hd * LANES, (hd + 1) * LANES)
        s = _dot_nt(k_ref[0, rows, hcols], q_ref[0, q0:, hcols])
        if masked:
            key = lax.broadcasted_iota(jnp.int32, s.shape, 0)
            qry = lax.broadcasted_iota(jnp.int32, s.shape, 1)
            s = jnp.where(key <= qry, s, -1e30)
        if m is None:
            m_new = jnp.max(s, axis=0, keepdims=True)
            p = jnp.exp2(s - m_new)
            return p.astype(BF16), m_new, jnp.sum(p, axis=0, keepdims=True), None
        m_new = jnp.maximum(m[:, q0:], jnp.max(s, axis=0, keepdims=True))
        alpha = jnp.exp2(m[:, q0:] - m_new)
        p = jnp.exp2(s - m_new)
        l_new = alpha * l[:, q0:] + jnp.sum(p, axis=0, keepdims=True)
        return p.astype(BF16), lanes_from(q0, m, m_new), lanes_from(q0, l, l_new), alpha

    def all_heads(rows, state, q0=0, masked=False):
        ms, ls, accs = state
        new_ms, new_ls, new_accs = [], [], []
        for pair in range(n_pairs):
            vrows = slice(pair * LANES, (pair + 1) * LANES)
            lo, hi = 2 * pair, 2 * pair + 1
            p_lo, m_lo, l_lo, a_lo = probs(lo, rows, ms[lo], ls[lo], q0, masked)
            p_hi, m_hi, l_hi, a_hi = probs(hi, rows, ms[hi], ls[hi], q0, masked)
            pv = _dot(vt_lo_ref[0, vrows, rows], p_lo) + _dot(vt_hi_ref[0, vrows, rows], p_hi)
            if accs[pair] is not None:
                pv = lanes_from(q0, accs[pair],
                                accs[pair][:, q0:] * jnp.where(even_rows, a_lo, a_hi) + pv)
            new_ms += [m_lo, m_hi]
            new_ls += [l_lo, l_hi]
            new_accs.append(pv)
        return tuple(new_ms), tuple(new_ls), tuple(new_accs)

    none = (None,) * MLA_HEADS
    state = (none, none, none[:n_pairs])
    for d in range(diag_blocks):
        rows = pl.ds(pl.multiple_of((qi * diag_blocks + d) * blk, blk), blk)
        state = all_heads(rows, state, q0=d * blk, masked=True)

    def step(t, state):
        return all_heads(pl.ds(pl.multiple_of(t * blk, blk), blk), state)

    _, ls, accs = lax.fori_loop(0, qi * diag_blocks, step, state)
    for pair in range(n_pairs):
        denom = jnp.where(even_rows, ls[2 * pair], ls[2 * pair + 1])
        o_ref[0, :, pair * LANES:(pair + 1) * LANES] = (accs[pair] / denom).T.astype(BF16)


def _mla_attention(q, k, vt_lo, vt_hi):
    b, s, wq = q.shape
    wv = vt_lo.shape[1]
    return pl.pallas_call(
        _mla_kernel,
        grid=(b, s // MLA_Q_BLOCK),
        in_specs=[
            pl.BlockSpec((1, MLA_Q_BLOCK, wq), lambda bi, qi: (bi, qi, 0)),
            pl.BlockSpec((1, s, wq), lambda bi, qi: (bi, 0, 0)),
            pl.BlockSpec((1, wv, s), lambda bi, qi: (bi, 0, 0)),
            pl.BlockSpec((1, wv, s), lambda bi, qi: (bi, 0, 0)),
        ],
        out_specs=pl.BlockSpec((1, MLA_Q_BLOCK, wv), lambda bi, qi: (bi, qi, 0)),
        out_shape=jax.ShapeDtypeStruct((b, s, wv), BF16),
        compiler_params=pltpu.CompilerParams(
            dimension_semantics=("parallel", "arbitrary"), vmem_limit_bytes=VMEM_LIMIT),
        name="mla_attention",
    )(q, k, vt_lo, vt_hi)


def _postmix_kernel(x_ref, ysb_ref, ymla_ref, ycv_ref, p_ref, wo_ref, gmix_ref,
                    gpre_ref, wg_ref, wu_ref, wd_ref, gpost_ref,
                    gple_pre_ref, wgate_ref, wproj_ref, gple_post_ref,
                    o_ref, h_ref, acc_ref):
    sb_rows = SB_WIDTH
    mla_rows = SB_WIDTH + MLA_HEADS * HEAD_DIM
    mixed = (_dot(ysb_ref[...], wo_ref[:sb_rows, :])
             + _dot(ymla_ref[...], wo_ref[sb_rows:mla_rows, :])
             + _dot(ycv_ref[...], wo_ref[mla_rows:, :]))
    x = x_ref[...] + _rms(mixed, gmix_ref[...])
    x = _ffn_body(x, gpre_ref[...], wg_ref, wu_ref, wd_ref, gpost_ref[...], h_ref, acc_ref)
    h = _rms(x, gple_pre_ref[...]).astype(BF16)
    e = jax.nn.sigmoid(_dot(h, wgate_ref[...])) * _dot(p_ref[...].astype(BF16), wproj_ref[...])
    o_ref[...] = x + _rms(e, gple_post_ref[...])


def _postmix(x, layer, ysb, ymla, ycv, p, wo, gmix, ffn_params, gple_pre, wgate, wproj, gple_post):
    n = x.shape[0]
    tm = TOKEN_TILE
    return pl.pallas_call(
        _postmix_kernel,
        grid=(n // tm,),
        in_specs=[
            _row_spec(tm, D_MODEL),
            _row_spec(tm, SB_WIDTH),
            _row_spec(tm, MLA_HEADS * HEAD_DIM),
            _row_spec(tm, CONV_WIDTH),
            pl.BlockSpec((None, tm, PLE_DIM), lambda i: (layer, i, 0)),
            _layer_spec(layer, (D_MODEL, D_MODEL)),
            _layer_spec(layer, (1, D_MODEL)),
            *_ffn_weight_specs(layer),
            _layer_spec(layer, (1, D_MODEL)),
            _layer_spec(layer, (D_MODEL, D_MODEL)),
            _layer_spec(layer, (PLE_DIM, D_MODEL)),
            _layer_spec(layer, (1, D_MODEL)),
        ],
        out_specs=_row_spec(tm, D_MODEL),
        out_shape=jax.ShapeDtypeStruct((n, D_MODEL), F32),
        scratch_shapes=[pltpu.VMEM((tm, D_MODEL), BF16), pltpu.VMEM((tm, D_MODEL), F32)],
        compiler_params=pltpu.CompilerParams(
            dimension_semantics=("parallel",), vmem_limit_bytes=VMEM_LIMIT),
        name="postmix",
    )(x, ysb, ymla, ycv, p, wo, gmix, *ffn_params, gple_pre, wgate, wproj, gple_post)


def _ffn_params(g_pre, w_gate, w_up, w_down, g_post):
    return (g_pre[:, None, :], w_gate.astype(BF16), w_up.astype(BF16), w_down.astype(BF16),
            g_post[:, None, :])


def _pack_in(w_in):
    kr_at = _CV
    kr = w_in[..., kr_at:kr_at + MLA_ROPE]
    kr_tile = jnp.pad(kr, ((0, 0), (0, 0), (MLA_NOPE, LANES - MLA_NOPE - MLA_ROPE)))
    return jnp.concatenate([w_in[..., :kr_at], w_in[..., kr_at + MLA_ROPE:], kr_tile],
                           axis=-1).astype(BF16)


def _pack_uq(w_uq):
    depth = w_uq.shape[0]
    per_head = MLA_NOPE + MLA_ROPE
    w = w_uq.reshape(depth, MLA_Q_RANK, MLA_HEADS, per_head)
    w = jnp.pad(w, ((0, 0), (0, 0), (0, 0), (0, LANES - per_head)))
    return w.reshape(depth, MLA_Q_RANK, MLA_HEADS * LANES).astype(BF16)


def _pack_ukv(w_ukv):
    depth = w_ukv.shape[0]
    w = w_ukv.reshape(depth, MLA_KV_RANK, MLA_HEADS, MLA_NOPE + HEAD_DIM)
    wk = jnp.pad(w[..., :MLA_NOPE], ((0, 0), (0, 0), (0, 0), (0, LANES - MLA_NOPE)))
    wv = w[..., MLA_NOPE:]
    return jnp.concatenate([wk.reshape(depth, MLA_KV_RANK, MLA_HEADS * LANES),
                            wv.reshape(depth, MLA_KV_RANK, MLA_HEADS * HEAD_DIM)],
                           axis=-1).astype(BF16)


def kernel(x, p, positions, g_ffn1_pre, w1_gate, w1_up, w1_down, g_ffn1_post, g_mix_pre, w_in, g_mla_q, w_mla_uq, g_mla_kv, w_mla_ukv, w_conv, w_out, g_mix_post, g_ffn2_pre, w2_gate, w2_up, w2_down, g_ffn2_post, g_ple_pre, w_ple_gate, w_ple_proj, g_ple_post):
    batch, seq, _ = x.shape
    depth = w_in.shape[0]
    n = batch * seq
    assert seq % TOKEN_TILE == 0 and seq % ATT_BLOCK == 0

    half = MLA_ROPE // 2
    inv = ROPE_BASE ** (-jnp.arange(half, dtype=F32) / half)
    zeros = jnp.zeros
    inv_lane = jnp.concatenate([zeros(MLA_NOPE, F32), inv, inv, zeros(LANES - MLA_NOPE - MLA_ROPE, F32)])
    sgn_lane = jnp.concatenate([zeros(MLA_NOPE, F32), -jnp.ones(half, F32), jnp.ones(half, F32),
                                zeros(LANES - MLA_NOPE - MLA_ROPE, F32)])
    cos_t, sin_t = _rope_tables(positions.reshape(n, 1), inv_lane[None, :], sgn_lane[None, :])

    rows = lambda g: g[:, None, :]
    ffn1 = _ffn_params(g_ffn1_pre, w1_gate, w1_up, w1_down, g_ffn1_post)
    ffn2 = _ffn_params(g_ffn2_pre, w2_gate, w2_up, w2_down, g_ffn2_post)
    win, wq, wkv = _pack_in(w_in), _pack_uq(w_mla_uq), _pack_ukv(w_mla_ukv)
    wo, wgate, wproj = w_out.astype(BF16), w_ple_gate.astype(BF16), w_ple_proj.astype(BF16)
    p_rows = p.reshape(depth, n, PLE_DIM)

    xf = x.reshape(n, D_MODEL)
    for i in range(depth):
        (xf, sbq_lo, sbq_hi, sbk, sbv_lo, sbv_hi, mq, mk, mvt_lo, mvt_hi, ycv) = _premix(
            xf, i, ffn1, rows(g_mix_pre), win, rows(g_mla_q), wq, rows(g_mla_kv), wkv, w_conv,
            cos_t, sin_t, seq)
        seqs = lambda t: t.reshape(batch, seq, t.shape[-1])
        ysb = _sb_attention(seqs(sbq_lo), seqs(sbq_hi), seqs(sbk), seqs(sbv_lo), seqs(sbv_hi))
        ymla = _mla_attention(seqs(mq), seqs(mk), mvt_lo, mvt_hi)
        xf = _postmix(
            xf, i, ysb.reshape(n, -1), ymla.reshape(n, -1), ycv, p_rows, wo, rows(g_mix_post), ffn2,
            rows(g_ple_pre), wgate, wproj, rows(g_ple_post))
    return xf.reshape(batch, seq, D_MODEL)
```
